```python
import jax, jax.numpy as jnp
from jax import lax
import numpy as np

D_MODEL = 1024
BATCH = 2
SEQ = 16384
DEPTH = 2
DEC_BATCH = 32
DEC_SEQ = 2048
PAST_LEN = 128

N_MIXERS = 2
N_HEADS = 16
N_KV_HEADS = 4
HEAD_DIM = D_MODEL // N_HEADS
GQA_GROUP = N_HEADS // N_KV_HEADS
Q_DIM = N_HEADS * HEAD_DIM
KV_DIM = N_KV_HEADS * HEAD_DIM
QKV_DIM = Q_DIM + 2 * KV_DIM
ROPE_AXIS_DIM = HEAD_DIM // 2
ROPE_THETA = 10000.0
GRID_W = 64
Q_BLOCK = 128
CONV_WIDTH = 3
CONV_PAD = CONV_WIDTH // 2
D_FF = 4 * D_MODEL
NORM_EPS = 1e-6
N_ATTN_LAYERS = (DEPTH + 1) // 2
N_CONV_LAYERS = DEPTH // 2

kernel_name = "hybrid_axial_gqa_shortconv_encoder"


def rmsnorm(x, g):
    xf = x.astype(jnp.float32)
    y = xf * lax.rsqrt(jnp.mean(xf * xf, axis=-1, keepdims=True) + NORM_EPS)
    return (y * g.astype(jnp.float32)).astype(x.dtype)


def axial_rope_tables(length):
    rows = length // GRID_W
    row_ids = jnp.repeat(jnp.arange(rows, dtype=jnp.float32), GRID_W)
    col_ids = jnp.tile(jnp.arange(GRID_W, dtype=jnp.float32), rows)
    inv_freq = ROPE_THETA ** (-jnp.arange(0, ROPE_AXIS_DIM, 2, dtype=jnp.float32) / ROPE_AXIS_DIM)
    ang = jnp.concatenate([row_ids[:, None] * inv_freq[None, :],
                           col_ids[:, None] * inv_freq[None, :]], axis=-1)
    return jnp.cos(ang), jnp.sin(ang)


def apply_rope(x, cos, sin):
    xf = x.astype(jnp.float32).reshape(*x.shape[:-1], HEAD_DIM // 2, 2)
    x0, x1 = xf[..., 0], xf[..., 1]
    c = cos[None, :, None, :]
    s = sin[None, :, None, :]
    out = jnp.stack([x0 * c - x1 * s, x0 * s + x1 * c], axis=-1)
    return out.reshape(x.shape).astype(x.dtype)


def axial_gqa_attention(h, w_qkv, q_gain, k_gain, w_o, cos, sin):
    bsz, length, _ = h.shape
    qkv = h @ w_qkv
    q = qkv[..., :Q_DIM].reshape(bsz, length, N_HEADS, HEAD_DIM)
    k = qkv[..., Q_DIM:Q_DIM + KV_DIM].reshape(bsz, length, N_KV_HEADS, HEAD_DIM)
    v = qkv[..., Q_DIM + KV_DIM:].reshape(bsz, length, N_KV_HEADS, HEAD_DIM)
    q = apply_rope(rmsnorm(q, q_gain), cos, sin) * (HEAD_DIM ** -0.5)
    k = apply_rope(rmsnorm(k, k_gain), cos, sin)
    n_blocks = length // Q_BLOCK
    qb = q.reshape(bsz, n_blocks, Q_BLOCK, N_KV_HEADS, GQA_GROUP, HEAD_DIM).transpose(1, 0, 2, 3, 4, 5)

    def one_block(q_blk):
        s = jnp.einsum('bqkgd,bskd->bkgqs', q_blk, k)
        p = jax.nn.softmax(s.astype(jnp.float32), axis=-1).astype(v.dtype)
        return jnp.einsum('bkgqs,bskd->bqkgd', p, v)

    o = lax.map(one_block, qb)
    o = o.transpose(1, 0, 2, 3, 4, 5).reshape(bsz, length, Q_DIM)
    return o @ w_o


def short_gated_conv(h, w_in, conv_w, w_out):
    bcx = h @ w_in
    b_gate = bcx[..., :D_MODEL]
    c_gate = bcx[..., D_MODEL:2 * D_MODEL]
    xv = bcx[..., 2 * D_MODEL:]
    u = c_gate * xv
    length = u.shape[1]
    up = jnp.pad(u, ((0, 0), (CONV_PAD, CONV_PAD), (0, 0)))
    z = up[:, 0:length] * conv_w[0]
    for tap in range(1, CONV_WIDTH):
        z = z + up[:, tap:tap + length] * conv_w[tap]
    return (b_gate * z) @ w_out


def sqrelu_mlp(h, w1, w2):
    return jnp.square(jax.nn.relu(h @ w1)) @ w2


def trunk(x, norm_mix, norm_mlp, w_qkv, q_gain, k_gain, w_o,
          w_conv_in, conv_w, w_conv_out, w_mlp1, w_mlp2, norm_final):
    length = x.shape[1]
    cos, sin = axial_rope_tables(length)
    for i in range(DEPTH):
        h = rmsnorm(x, norm_mix[i])
        j = i // N_MIXERS
        if i % N_MIXERS == 0:
            x = x + axial_gqa_attention(h, w_qkv[j], q_gain[j], k_gain[j], w_o[j], cos, sin)
        else:
            x = x + short_gated_conv(h, w_conv_in[j], conv_w[j], w_conv_out[j])
        x = x + sqrelu_mlp(rmsnorm(x, norm_mlp[i]), w_mlp1[i], w_mlp2[i])
    return rmsnorm(x, norm_final)


def setup_inputs(seed: int = 0) -> dict:
    key = jax.random.key(seed)
    ks = jax.random.split(key, 16)
    f32 = jnp.float32
    nrm = lambda k, shape, fan_in: jax.random.normal(k, shape, f32) * (fan_in ** -0.5)
    gain = lambda k, shape: 1.0 + 0.05 * jax.random.normal(k, shape, f32)
    return {
        "x_prompt": jax.random.normal(ks[0], (BATCH, SEQ, D_MODEL), f32),
        "x_sample": jax.random.normal(ks[1], (DEC_BATCH, DEC_SEQ, D_MODEL), f32),
        "norm_mix": gain(ks[2], (DEPTH, D_MODEL)),
        "norm_mlp": gain(ks[3], (DEPTH, D_MODEL)),
        "w_qkv": nrm(ks[4], (N_ATTN_LAYERS, D_MODEL, QKV_DIM), D_MODEL),
        "q_gain": gain(ks[5], (N_ATTN_LAYERS, HEAD_DIM)),
        "k_gain": gain(ks[6], (N_ATTN_LAYERS, HEAD_DIM)),
        "w_o": nrm(ks[7], (N_ATTN_LAYERS, Q_DIM, D_MODEL), Q_DIM),
        "w_conv_in": nrm(ks[8], (N_CONV_LAYERS, D_MODEL, 3 * D_MODEL), D_MODEL),
        "conv_w": nrm(ks[9], (N_CONV_LAYERS, CONV_WIDTH, D_MODEL), CONV_WIDTH),
        "w_conv_out": nrm(ks[10], (N_CONV_LAYERS, D_MODEL, D_MODEL), D_MODEL),
        "w_mlp1": nrm(ks[11], (DEPTH, D_MODEL, D_FF), D_MODEL),
        "w_mlp2": nrm(ks[12], (DEPTH, D_FF, D_MODEL), D_FF),
        "norm_final": gain(ks[13], (D_MODEL,)),
    }


def reference(x_prompt, x_sample, norm_mix, norm_mlp, w_qkv, q_gain, k_gain, w_o,
              w_conv_in, conv_w, w_conv_out, w_mlp1, w_mlp2, norm_final):
    y_prompt = trunk(x_prompt, norm_mix, norm_mlp, w_qkv, q_gain, k_gain, w_o,
                     w_conv_in, conv_w, w_conv_out, w_mlp1, w_mlp2, norm_final)
    y_sample = trunk(x_sample, norm_mix, norm_mlp, w_qkv, q_gain, k_gain, w_o,
                     w_conv_in, conv_w, w_conv_out, w_mlp1, w_mlp2, norm_final)
    return (y_prompt, y_sample)
```

```python
import functools

import numpy as np
import jax
import jax.numpy as jnp
from jax import lax
from jax.experimental import pallas as pl
from jax.experimental.pallas import tpu as pltpu

D_MODEL = 1024
N_HEADS = 16
N_KV_HEADS = 4
HEAD_DIM = 64
GQA_GROUP = N_HEADS // N_KV_HEADS
HALF_DIM = HEAD_DIM // 2
KV_DIM = N_KV_HEADS * HEAD_DIM
Q_DIM = N_HEADS * HEAD_DIM
D_FF = 4 * D_MODEL
ROPE_AXIS_DIM = HEAD_DIM // 2
ROPE_THETA = 10000.0
GRID_W = 64
NORM_EPS = 1e-6
CONV_WIDTH = 3

LANE = 128
SUBLANE = 8
HALF_SLAB = N_KV_HEADS * LANE
V_SLAB = N_KV_HEADS * LANE
QKV_COLS = 4 * HALF_SLAB + V_SLAB
VMEM_LIMIT_BYTES = 56 * 1024 * 1024

ROW_TILE = 512
FF_CHUNK = 1024
ATT_TQ = 256
ATT_TKB = 2048
ATT_TK = 512

BF16 = jnp.bfloat16
F32 = jnp.float32


def _const_spec(shape):
    zeros = (0,) * len(shape)
    return pl.BlockSpec(shape, lambda *_: zeros, pipeline_mode=pl.Buffered(1))


def _rms(x, gain):
    return x * lax.rsqrt(jnp.mean(x * x, axis=-1, keepdims=True) + NORM_EPS) * gain


def _qkv_kernel(x_ref, g_ref, w_ref, eq_ref, ek_ref, hg_ref, cos_ref, sin_ref,
                q_ref, k_ref, v_ref):
    h = _rms(x_ref[...], g_ref[...]).astype(BF16)
    y = jnp.dot(h, w_ref[...], preferred_element_type=F32)
    cos = jnp.concatenate([cos_ref[...]] * N_KV_HEADS, axis=1)
    sin = jnp.concatenate([sin_ref[...]] * N_KV_HEADS, axis=1)

    def norm_rope(a0, a1, e_ref, g0, g1, scale, out_ref):
        sq = a0 * a0 + a1 * a1
        hi = sq.astype(BF16)
        lo = (sq - hi.astype(F32)).astype(BF16)
        ss = (jnp.dot(hi, e_ref[...], preferred_element_type=F32)
              + jnp.dot(lo, e_ref[...], preferred_element_type=F32))
        r = lax.rsqrt(ss * (1.0 / HEAD_DIM) + NORM_EPS)
        n0 = a0 * r * g0
        n1 = a1 * r * g1
        o0 = ((n0 * cos - n1 * sin) * scale).astype(BF16)
        o1 = ((n0 * sin + n1 * cos) * scale).astype(BF16)
        for t in range(N_KV_HEADS):
            out_ref[:, 2 * t * LANE:(2 * t + 1) * LANE] = o0[:, t * LANE:(t + 1) * LANE]
            out_ref[:, (2 * t + 1) * LANE:(2 * t + 2) * LANE] = o1[:, t * LANE:(t + 1) * LANE]

    S = HALF_SLAB
    norm_rope(y[:, 0:S], y[:, S:2 * S], eq_ref, hg_ref[0:1, :], hg_ref[1:2, :],
              HEAD_DIM ** -0.5, q_ref)
    norm_rope(y[:, 2 * S:3 * S], y[:, 3 * S:4 * S], ek_ref, hg_ref[2:3, :], hg_ref[3:4, :],
              1.0, k_ref)
    yv = y[:, 4 * S:]
    lane = lax.broadcasted_iota(jnp.int32, yv.shape, 1)
    v_ref[...] = jnp.where(lane % LANE == HEAD_DIM, 1.0, yv).astype(BF16)


def _qkv_call(x2d, gain, w_big, e_q, e_k, head_gains, cos, sin, seq_len):
    T = x2d.shape[0]
    tm = ROW_TILE
    tiles_per_seq = seq_len // tm
    row = lambda i: (i, 0)
    pos = lambda i: (i % tiles_per_seq, 0)
    return pl.pallas_call(
        _qkv_kernel,
        grid=(T // tm,),
        in_specs=[
            pl.BlockSpec((tm, D_MODEL), row),
            _const_spec((1, D_MODEL)),
            _const_spec((D_MODEL, QKV_COLS)),
            _const_spec((HALF_SLAB, HALF_SLAB)),
            _const_spec((HALF_SLAB, HALF_SLAB)),
            _const_spec((4, HALF_SLAB)),
            pl.BlockSpec((tm, LANE), pos),
            pl.BlockSpec((tm, LANE), pos),
        ],
        out_specs=[
            pl.BlockSpec((tm, 2 * HALF_SLAB), row),
            pl.BlockSpec((tm, 2 * HALF_SLAB), row),
            pl.BlockSpec((tm, V_SLAB), row),
        ],
        out_shape=[
            jax.ShapeDtypeStruct((T, 2 * HALF_SLAB), BF16),
            jax.ShapeDtypeStruct((T, 2 * HALF_SLAB), BF16),
            jax.ShapeDtypeStruct((T, V_SLAB), BF16),
        ],
        compiler_params=pltpu.CompilerParams(
            dimension_semantics=("parallel",), vmem_limit_bytes=VMEM_LIMIT_BYTES),
        name="qkv_norm_rope",
    )(x2d, gain, w_big, e_q, e_k, head_gains, cos, sin)


def _attn_kernel(q_ref, k_ref, v_ref, o_ref, qm_ref, m_ref, acc_ref, *, tq, tk, n_chunks):
    j = pl.program_id(3)

    @pl.when(j == 0)
    def _():
        q = q_ref[...]
        lane = lax.broadcasted_iota(jnp.int32, q.shape, 1)
        head_of_lane = (lane % LANE) // HALF_DIM
        for g in range(GQA_GROUP):
            qm_ref[g * tq:(g + 1) * tq, :] = jnp.where(head_of_lane == g, q, jnp.zeros_like(q))
        m_ref[...] = jnp.full(m_ref.shape, -jnp.inf, F32)
        acc_ref[...] = jnp.zeros(acc_ref.shape, F32)

    def chunk(c, carry):
        start = pl.multiple_of(c * tk, tk)
        kc = k_ref[pl.ds(start, tk), :]
        vc = v_ref[pl.ds(start, tk), :]
        s = lax.dot_general(qm_ref[...], kc, (((1,), (1,)), ((), ())),
                            preferred_element_type=F32)
        m_prev = m_ref[...]
        m_new = jnp.maximum(m_prev, jnp.max(s, axis=1, keepdims=True))
        alpha = jnp.exp(m_prev - m_new)
        p = jnp.exp(s - m_new).astype(BF16)
        acc_ref[...] = alpha * acc_ref[...] + jnp.dot(p, vc, preferred_element_type=F32)
        m_ref[...] = m_new
        return carry

    lax.fori_loop(0, n_chunks, chunk, 0)

    @pl.when(j == pl.num_programs(3) - 1)
    def _():
        acc = acc_ref[...]
        o = acc / acc[:, HEAD_DIM:HEAD_DIM + 1]
        lane = lax.broadcasted_iota(jnp.int32, (tq, LANE), 1)
        pairs = []
        for g in range(0, GQA_GROUP, 2):
            lo = o[g * tq:(g + 1) * tq, :]
            hi = pltpu.roll(o[(g + 1) * tq:(g + 2) * tq, :], HEAD_DIM, 1)
            pairs.append(jnp.where(lane < HEAD_DIM, lo, hi))
        o_ref[...] = jnp.concatenate(pairs, axis=1).astype(BF16)


def _attn_call(q, k, v):
    B, L, _ = q.shape
    tq = min(ATT_TQ, L)
    tkb = min(ATT_TKB, L)
    tk = min(ATT_TK, tkb)
    kern = functools.partial(_attn_kernel, tq=tq, tk=tk, n_chunks=tkb // tk)
    return pl.pallas_call(
        kern,
        grid=(B, N_KV_HEADS, L // tq, L // tkb),
        in_specs=[
            pl.BlockSpec((None, tq, 2 * LANE), lambda b, t, i, j: (b, i, t)),
            pl.BlockSpec((None, tkb, 2 * LANE), lambda b, t, i, j: (b, j, t)),
            pl.BlockSpec((None, tkb, LANE), lambda b, t, i, j: (b, j, t)),
        ],
        out_specs=pl.BlockSpec((None, tq, GQA_GROUP * HEAD_DIM), lambda b, t, i, j: (b, i, t)),
        out_shape=jax.ShapeDtypeStruct((B, L, Q_DIM), BF16),
        scratch_shapes=[
            pltpu.VMEM((GQA_GROUP * tq, 2 * LANE), BF16),
            pltpu.VMEM((GQA_GROUP * tq, 1), F32),
            pltpu.VMEM((GQA_GROUP * tq, LANE), F32),
        ],
        compiler_params=pltpu.CompilerParams(
            dimension_semantics=("parallel", "parallel", "parallel", "arbitrary"),
            vmem_limit_bytes=VMEM_LIMIT_BYTES),
        name="flash_attn",
    )(q, k, v)


def _mlp_body(x1, gm, w1_ref, w2_ref):
    h = _rms(x1, gm).astype(BF16)
    acc = jnp.zeros_like(x1)
    for c in range(D_FF // FF_CHUNK):
        cols = slice(c * FF_CHUNK, (c + 1) * FF_CHUNK)
        hid = jnp.maximum(jnp.dot(h, w1_ref[:, cols], preferred_element_type=F32), 0.0)
        acc = acc + jnp.dot((hid * hid).astype(BF16), w2_ref[cols, :], preferred_element_type=F32)
    return x1 + acc


def _attn_out_mlp_kernel(x_ref, o_ref, wo_ref, gm_ref, w1_ref, w2_ref, out_ref):
    x1 = x_ref[...] + jnp.dot(o_ref[...], wo_ref[...], preferred_element_type=F32)
    out_ref[...] = _mlp_body(x1, gm_ref[...], w1_ref, w2_ref)


def _mlp_final_kernel(x_ref, gm_ref, w1_ref, w2_ref, gf_ref, out_ref):
    y = _mlp_body(x_ref[...], gm_ref[...], w1_ref, w2_ref)
    out_ref[...] = _rms(y, gf_ref[...])


def _row_call(kernel, name, T, row_inputs, const_inputs):
    tm = ROW_TILE
    row = lambda i: (i, 0)
    in_specs = [pl.BlockSpec((tm, a.shape[1]), row) for a in row_inputs]
    in_specs += [_const_spec(a.shape) for a in const_inputs]
    return pl.pallas_call(
        kernel,
        grid=(T // tm,),
        in_specs=in_specs,
        out_specs=pl.BlockSpec((tm, D_MODEL), row),
        out_shape=jax.ShapeDtypeStruct((T, D_MODEL), F32),
        compiler_params=pltpu.CompilerParams(
            dimension_semantics=("parallel",), vmem_limit_bytes=VMEM_LIMIT_BYTES),
        name=name,
    )(*row_inputs, *const_inputs)


def _conv_kernel(x_ref, xp_ref, xn_ref, g_ref, win_ref, cw_ref, wout_ref, out_ref, u_ref,
                 *, tm, tiles_per_seq):
    i = pl.program_id(0)
    first = (i % tiles_per_seq) == 0
    last = (i % tiles_per_seq) == tiles_per_seq - 1
    x = x_ref[...]
    xa = jnp.concatenate([xp_ref[...], x, xn_ref[...]], axis=0)
    h = _rms(xa, g_ref[...]).astype(BF16)
    bcx = jnp.dot(h, win_ref[...], preferred_element_type=F32)
    u = bcx[:, D_MODEL:2 * D_MODEL] * bcx[:, 2 * D_MODEL:]
    r = lax.broadcasted_iota(jnp.int32, (tm + 2 * SUBLANE, 1), 0)
    pad = (first & (r < SUBLANE)) | (last & (r >= tm + SUBLANE))
    u_ref[...] = jnp.where(pad, 0.0, u)
    z = (u_ref[pl.ds(SUBLANE - 1, tm), :] * cw_ref[0:1, :]
         + u_ref[pl.ds(SUBLANE, tm), :] * cw_ref[1:2, :]
         + u_ref[pl.ds(SUBLANE + 1, tm), :] * cw_ref[2:3, :])
    gated = (bcx[SUBLANE:SUBLANE + tm, :D_MODEL] * z).astype(BF16)
    out_ref[...] = x + jnp.dot(gated, wout_ref[...], preferred_element_type=F32)


def _conv_call(x2d, gain, w_in, conv_w, w_out, seq_len):
    T = x2d.shape[0]
    tm = ROW_TILE
    halo_blocks = tm // SUBLANE
    n_halo = T // SUBLANE
    kern = functools.partial(_conv_kernel, tm=tm, tiles_per_seq=seq_len // tm)
    return pl.pallas_call(
        kern,
        grid=(T // tm,),
        in_specs=[
            pl.BlockSpec((tm, D_MODEL), lambda i: (i, 0)),
            pl.BlockSpec((SUBLANE, D_MODEL), lambda i: (jnp.maximum(i * halo_blocks - 1, 0), 0)),
            pl.BlockSpec((SUBLANE, D_MODEL),
                         lambda i: (jnp.minimum((i + 1) * halo_blocks, n_halo - 1), 0)),
            _const_spec((1, D_MODEL)),
            _const_spec((D_MODEL, 3 * D_MODEL)),
            _const_spec((CONV_WIDTH, D_MODEL)),
            _const_spec((D_MODEL, D_MODEL)),
        ],
        out_specs=pl.BlockSpec((tm, D_MODEL), lambda i: (i, 0)),
        out_shape=jax.ShapeDtypeStruct((T, D_MODEL), F32),
        scratch_shapes=[pltpu.VMEM((tm + 2 * SUBLANE, D_MODEL), F32)],
        compiler_params=pltpu.CompilerParams(
            dimension_semantics=("parallel",), vmem_limit_bytes=VMEM_LIMIT_BYTES),
        name="gated_conv",
    )(x2d, x2d, x2d, gain, w_in, conv_w, w_out)


def _prep_attn_weights(w_qkv, q_gain, k_gain):
    t = np.arange(N_KV_HEADS)[:, None, None]
    g = np.arange(GQA_GROUP)[None, :, None]
    i = np.arange(HALF_DIM)[None, None, :]
    q0 = ((t * GQA_GROUP + g) * HEAD_DIM + 2 * i).reshape(-1)
    k0 = np.broadcast_to(Q_DIM + t * HEAD_DIM + 2 * i, (N_KV_HEADS, GQA_GROUP, HALF_DIM)).reshape(-1)
    cols = np.concatenate([q0, q0 + 1, k0, k0 + 1])
    w_main = jnp.take(w_qkv, jnp.asarray(cols), axis=1)
    w_v = w_qkv[:, Q_DIM + KV_DIM:].reshape(D_MODEL, N_KV_HEADS, HEAD_DIM)
    w_v = jnp.pad(w_v, ((0, 0), (0, 0), (0, LANE - HEAD_DIM))).reshape(D_MODEL, V_SLAB)
    w_big = jnp.concatenate([w_main, w_v], axis=1).astype(BF16)

    lane = np.arange(HALF_SLAB)
    e_q = (lane[:, None] // HALF_DIM == lane[None, :] // HALF_DIM).astype(np.float32)
    e_k = (lane[:, None] // LANE == lane[None, :] // LANE).astype(np.float32) / GQA_GROUP
    reps = HALF_SLAB // HALF_DIM
    head_gains = jnp.stack([jnp.tile(q_gain[0::2], reps), jnp.tile(q_gain[1::2], reps),
                            jnp.tile(k_gain[0::2], reps), jnp.tile(k_gain[1::2], reps)])
    return w_big, jnp.asarray(e_q, BF16), jnp.asarray(e_k, BF16), head_gains


def _rope_tables(length):
    rows = length // GRID_W
    row_ids = jnp.repeat(jnp.arange(rows, dtype=F32), GRID_W)
    col_ids = jnp.tile(jnp.arange(GRID_W, dtype=F32), rows)
    inv_freq = ROPE_THETA ** (-jnp.arange(0, ROPE_AXIS_DIM, 2, dtype=F32) / ROPE_AXIS_DIM)
    ang = jnp.concatenate([row_ids[:, None] * inv_freq[None, :],
                           col_ids[:, None] * inv_freq[None, :]], axis=-1)
    reps = LANE // HALF_DIM
    return jnp.tile(jnp.cos(ang), (1, reps)), jnp.tile(jnp.sin(ang), (1, reps))


def _trunk(x, p):
    B, L, _ = x.shape
    T = B * L
    x2d = x.reshape(T, D_MODEL)
    cos, sin = _rope_tables(L)
    q, k, v = _qkv_call(x2d, p["g_mix0"], p["w_big"], p["e_q"], p["e_k"], p["head_gains"],
                        cos, sin, L)
    o = _attn_call(q.reshape(B, L, -1), k.reshape(B, L, -1), v.reshape(B, L, -1))
    x2d = _row_call(_attn_out_mlp_kernel, "attn_out_mlp", T, [x2d, o.reshape(T, Q_DIM)],
                    [p["w_o"], p["g_mlp0"], p["w1_0"], p["w2_0"]])
    x2d = _conv_call(x2d, p["g_mix1"], p["w_in"], p["conv_w"], p["w_out"], L)
    y = _row_call(_mlp_final_kernel, "mlp_final_norm", T, [x2d],
                  [p["g_mlp1"], p["w1_1"], p["w2_1"], p["g_final"]])
    return y.reshape(B, L, D_MODEL)


def kernel(x_prompt, x_sample, norm_mix, norm_mlp, w_qkv, q_gain, k_gain, w_o, w_conv_in, conv_w,
           w_conv_out, w_mlp1, w_mlp2, norm_final):
    w_big, e_q, e_k, head_gains = _prep_attn_weights(w_qkv[0], q_gain[0], k_gain[0])
    p = {
        "g_mix0": norm_mix[0:1], "g_mix1": norm_mix[1:2],
        "g_mlp0": norm_mlp[0:1], "g_mlp1": norm_mlp[1:2],
        "g_final": norm_final[None, :],
        "w_big": w_big, "e_q": e_q, "e_k": e_k, "head_gains": head_gains,
        "w_o": w_o[0].astype(BF16),
        "w1_0": w_mlp1[0].astype(BF16), "w2_0": w_mlp2[0].astype(BF16),
        "w1_1": w_mlp1[1].astype(BF16), "w2_1": w_mlp2[1].astype(BF16),
        "w_in": w_conv_in[0].astype(BF16), "conv_w": conv_w[0],
        "w_out": w_conv_out[0].astype(BF16),
    }
    return (_trunk(x_prompt, p), _trunk(x_sample, p))
```

```python
import functools

import numpy as np
import jax
import jax.numpy as jnp
from jax import lax
from jax.experimental import pallas as pl
from jax.experimental.pallas import tpu as pltpu

D_MODEL = 1024
N_HEADS = 16
N_KV_HEADS = 4
HEAD_DIM = 64
GQA_GROUP = N_HEADS // N_KV_HEADS
HALF_DIM = HEAD_DIM // 2
KV_DIM = N_KV_HEADS * HEAD_DIM
Q_DIM = N_HEADS * HEAD_DIM
D_FF = 4 * D_MODEL
ROPE_AXIS_DIM = HEAD_DIM // 2
ROPE_THETA = 10000.0
GRID_W = 64
NORM_EPS = 1e-6
CONV_WIDTH = 3

LANE = 128
SUBLANE = 8
HALF_SLAB = N_KV_HEADS * LANE
V_SLAB = N_KV_HEADS * LANE
VMEM_LIMIT_BYTES = 56 * 1024 * 1024

ROW_TILE = 512
FF_CHUNK = 1024
ATT_TQ = 256
ATT_TKB = 2048
ATT_TK = 512

BF16 = jnp.bfloat16
F32 = jnp.float32


def _const_spec(shape):
    zeros = (0,) * len(shape)
    return pl.BlockSpec(shape, lambda *_: zeros, pipeline_mode=pl.Buffered(1))


def _rms(x, gain):
    return x * lax.rsqrt(jnp.mean(x * x, axis=-1, keepdims=True) + NORM_EPS) * gain


def _qkv_kernel(x_ref, g_ref, w_ref, wvt_ref, eq_ref, ek_ref, hg_ref, cos_ref, sin_ref,
                q_ref, k_ref, vt_ref):
    h = _rms(x_ref[...], g_ref[...]).astype(BF16)
    y = jnp.dot(h, w_ref[...], preferred_element_type=F32)
    cos = jnp.concatenate([cos_ref[...]] * N_KV_HEADS, axis=1)
    sin = jnp.concatenate([sin_ref[...]] * N_KV_HEADS, axis=1)

    def norm_rope(a0, a1, e_ref, g0, g1, scale, out_ref):
        sq = a0 * a0 + a1 * a1
        hi = sq.astype(BF16)
        lo = (sq - hi.astype(F32)).astype(BF16)
        ss = (jnp.dot(hi, e_ref[...], preferred_element_type=F32)
              + jnp.dot(lo, e_ref[...], preferred_element_type=F32))
        r = lax.rsqrt(ss * (1.0 / HEAD_DIM) + NORM_EPS)
        n0 = a0 * r * g0
        n1 = a1 * r * g1
        o0 = ((n0 * cos - n1 * sin) * scale).astype(BF16)
        o1 = ((n0 * sin + n1 * cos) * scale).astype(BF16)
        for t in range(N_KV_HEADS):
            out_ref[:, 2 * t * LANE:(2 * t + 1) * LANE] = o0[:, t * LANE:(t + 1) * LANE]
            out_ref[:, (2 * t + 1) * LANE:(2 * t + 2) * LANE] = o1[:, t * LANE:(t + 1) * LANE]

    S = HALF_SLAB
    norm_rope(y[:, 0:S], y[:, S:2 * S], eq_ref, hg_ref[0:1, :], hg_ref[1:2, :],
              HEAD_DIM ** -0.5, q_ref)
    norm_rope(y[:, 2 * S:3 * S], y[:, 3 * S:4 * S], ek_ref, hg_ref[2:3, :], hg_ref[3:4, :],
              1.0, k_ref)
    yvt = lax.dot_general(wvt_ref[...], h, (((1,), (1,)), ((), ())), preferred_element_type=F32)
    vrow = lax.broadcasted_iota(jnp.int32, yvt.shape, 0)
    vt_ref[...] = jnp.where(vrow % LANE == HEAD_DIM, 1.0, yvt).astype(BF16)


def _qkv_call(x2d, gain, w_big, w_vt, e_q, e_k, head_gains, cos, sin, seq_len):
    T = x2d.shape[0]
    tm = ROW_TILE
    tiles_per_seq = seq_len // tm
    row = lambda i: (i, 0)
    pos = lambda i: (i % tiles_per_seq, 0)
    return pl.pallas_call(
        _qkv_kernel,
        grid=(T // tm,),
        in_specs=[
            pl.BlockSpec((tm, D_MODEL), row),
            _const_spec((1, D_MODEL)),
            _const_spec((D_MODEL, 4 * HALF_SLAB)),
            _const_spec((V_SLAB, D_MODEL)),
            _const_spec((HALF_SLAB, HALF_SLAB)),
            _const_spec((HALF_SLAB, HALF_SLAB)),
            _const_spec((4, HALF_SLAB)),
            pl.BlockSpec((tm, LANE), pos),
            pl.BlockSpec((tm, LANE), pos),
        ],
        out_specs=[
            pl.BlockSpec((tm, 2 * HALF_SLAB), row),
            pl.BlockSpec((tm, 2 * HALF_SLAB), row),
            pl.BlockSpec((V_SLAB, tm), lambda i: (0, i)),
        ],
        out_shape=[
            jax.ShapeDtypeStruct((T, 2 * HALF_SLAB), BF16),
            jax.ShapeDtypeStruct((T, 2 * HALF_SLAB), BF16),
            jax.ShapeDtypeStruct((V_SLAB, T), BF16),
        ],
        compiler_params=pltpu.CompilerParams(
            dimension_semantics=("parallel",), vmem_limit_bytes=VMEM_LIMIT_BYTES),
        name="qkv_norm_rope",
    )(x2d, gain, w_big, w_vt, e_q, e_k, head_gains, cos, sin)


def _attn_kernel(q_ref, k_ref, vt_ref, o_ref, qm_ref, m_ref, acc_ref, *, tq, tk, n_chunks):
    j = pl.program_id(3)

    @pl.when(j == 0)
    def _():
        q = q_ref[...]
        lane = lax.broadcasted_iota(jnp.int32, q.shape, 1)
        head_of_lane = (lane % LANE) // HALF_DIM
        for g in range(GQA_GROUP):
            qm_ref[g * tq:(g + 1) * tq, :] = jnp.where(head_of_lane == g, q, jnp.zeros_like(q))
        m_ref[...] = jnp.full(m_ref.shape, -jnp.inf, F32)
        acc_ref[...] = jnp.zeros(acc_ref.shape, F32)

    def chunk(c, carry):
        start = pl.multiple_of(c * tk, tk)
        kc = k_ref[pl.ds(start, tk), :]
        vtc = vt_ref[:, pl.ds(start, tk)]
        s = lax.dot_general(kc, qm_ref[...], (((1,), (1,)), ((), ())),
                            preferred_element_type=F32)
        m_prev = m_ref[...]
        m_new = jnp.maximum(m_prev, jnp.max(s, axis=0, keepdims=True))
        alpha = jnp.exp(m_prev - m_new)
        p = jnp.exp(s - m_new).astype(BF16)
        acc_ref[...] = alpha * acc_ref[...] + jnp.dot(vtc, p, preferred_element_type=F32)
        m_ref[...] = m_new
        return carry

    lax.fori_loop(0, n_chunks, chunk, 0)

    @pl.when(j == pl.num_programs(3) - 1)
    def _():
        acc = acc_ref[...]
        o = (acc / acc[HEAD_DIM:HEAD_DIM + 1, :]).T
        lane = lax.broadcasted_iota(jnp.int32, (tq, LANE), 1)
        pairs = []
        for g in range(0, GQA_GROUP, 2):
            lo = o[g * tq:(g + 1) * tq, :]
            hi = pltpu.roll(o[(g + 1) * tq:(g + 2) * tq, :], HEAD_DIM, 1)
            pairs.append(jnp.where(lane < HEAD_DIM, lo, hi))
        o_ref[...] = jnp.concatenate(pairs, axis=1).astype(BF16)


def _attn_call(q, k, vt):
    B, L, _ = q.shape
    tq = min(ATT_TQ, L)
    tkb = min(ATT_TKB, L)
    tk = min(ATT_TK, tkb)
    kv_blocks = L // tkb
    kern = functools.partial(_attn_kernel, tq=tq, tk=tk, n_chunks=tkb // tk)
    return pl.pallas_call(
        kern,
        grid=(B, N_KV_HEADS, L // tq, kv_blocks),
        in_specs=[
            pl.BlockSpec((None, tq, 2 * LANE), lambda b, t, i, j: (b, i, t)),
            pl.BlockSpec((None, tkb, 2 * LANE), lambda b, t, i, j: (b, j, t)),
            pl.BlockSpec((LANE, tkb), lambda b, t, i, j: (t, b * kv_blocks + j)),
        ],
        out_specs=pl.BlockSpec((None, tq, GQA_GROUP * HEAD_DIM), lambda b, t, i, j: (b, i, t)),
        out_shape=jax.ShapeDtypeStruct((B, L, Q_DIM), BF16),
        scratch_shapes=[
            pltpu.VMEM((GQA_GROUP * tq, 2 * LANE), BF16),
            pltpu.VMEM((1, GQA_GROUP * tq), F32),
            pltpu.VMEM((LANE, GQA_GROUP * tq), F32),
        ],
        compiler_params=pltpu.CompilerParams(
            dimension_semantics=("parallel", "parallel", "parallel", "arbitrary"),
            vmem_limit_bytes=VMEM_LIMIT_BYTES),
        name="flash_attn",
    )(q, k, vt)


def _mlp_body(x1, gm, w1_ref, w2_ref):
    h = _rms(x1, gm).astype(BF16)
    acc = jnp.zeros_like(x1)
    for c in range(D_FF // FF_CHUNK):
        cols = slice(c * FF_CHUNK, (c + 1) * FF_CHUNK)
        hid = jnp.maximum(jnp.dot(h, w1_ref[:, cols], preferred_element_type=F32), 0.0)
        acc = acc + jnp.dot((hid * hid).astype(BF16), w2_ref[cols, :], preferred_element_type=F32)
    return x1 + acc


def _attn_out_mlp_kernel(x_ref, o_ref, wo_ref, gm_ref, w1_ref, w2_ref, out_ref):
    x1 = x_ref[...] + jnp.dot(o_ref[...], wo_ref[...], preferred_element_type=F32)
    out_ref[...] = _mlp_body(x1, gm_ref[...], w1_ref, w2_ref)


def _mlp_final_kernel(x_ref, gm_ref, w1_ref, w2_ref, gf_ref, out_ref):
    y = _mlp_body(x_ref[...], gm_ref[...], w1_ref, w2_ref)
    out_ref[...] = _rms(y, gf_ref[...])


def _row_call(kernel, name, T, row_inputs, const_inputs):
    tm = ROW_TILE
    row = lambda i: (i, 0)
    in_specs = [pl.BlockSpec((tm, a.shape[1]), row) for a in row_inputs]
    in_specs += [_const_spec(a.shape) for a in const_inputs]
    return pl.pallas_call(
        kernel,
        grid=(T // tm,),
        in_specs=in_specs,
        out_specs=pl.BlockSpec((tm, D_MODEL), row),
        out_shape=jax.ShapeDtypeStruct((T, D_MODEL), F32),
        compiler_params=pltpu.CompilerParams(
            dimension_semantics=("parallel",), vmem_limit_bytes=VMEM_LIMIT_BYTES),
        name=name,
    )(*row_inputs, *const_inputs)


def _conv_kernel(x_ref, xp_ref, xn_ref, g_ref, win_ref, cw_ref, wout_ref, out_ref, u_ref,
                 *, tm, tiles_per_seq):
    i = pl.program_id(0)
    first = (i % tiles_per_seq) == 0
    last = (i % tiles_per_seq) == tiles_per_seq - 1
    x = x_ref[...]
    xa = jnp.concatenate([xp_ref[...], x, xn_ref[...]], axis=0)
    h = _rms(xa, g_ref[...]).astype(BF16)
    bcx = jnp.dot(h, win_ref[...], preferred_element_type=F32)
    u = bcx[:, D_MODEL:2 * D_MODEL] * bcx[:, 2 * D_MODEL:]
    r = lax.broadcasted_iota(jnp.int32, (tm + 2 * SUBLANE, 1), 0)
    pad = (first & (r < SUBLANE)) | (last & (r >= tm + SUBLANE))
    u_ref[...] = jnp.where(pad, 0.0, u)
    z = (u_ref[pl.ds(SUBLANE - 1, tm), :] * cw_ref[0:1, :]
         + u_ref[pl.ds(SUBLANE, tm), :] * cw_ref[1:2, :]
         + u_ref[pl.ds(SUBLANE + 1, tm), :] * cw_ref[2:3, :])
    gated = (bcx[SUBLANE:SUBLANE + tm, :D_MODEL] * z).astype(BF16)
    out_ref[...] = x + jnp.dot(gated, wout_ref[...], preferred_element_type=F32)


def _conv_call(x2d, gain, w_in, conv_w, w_out, seq_len):
    T = x2d.shape[0]
    tm = ROW_TILE
    halo_blocks = tm // SUBLANE
    n_halo = T // SUBLANE
    kern = functools.partial(_conv_kernel, tm=tm, tiles_per_seq=seq_len // tm)
    return pl.pallas_call(
        kern,
        grid=(T // tm,),
        in_specs=[
            pl.BlockSpec((tm, D_MODEL), lambda i: (i, 0)),
            pl.BlockSpec((SUBLANE, D_MODEL), lambda i: (jnp.maximum(i * halo_blocks - 1, 0), 0)),
            pl.BlockSpec((SUBLANE, D_MODEL),
                         lambda i: (jnp.minimum((i + 1) * halo_blocks, n_halo - 1), 0)),
            _const_spec((1, D_MODEL)),
            _const_spec((D_MODEL, 3 * D_MODEL)),
            _const_spec((CONV_WIDTH, D_MODEL)),
            _const_spec((D_MODEL, D_MODEL)),
        ],
        out_specs=pl.BlockSpec((tm, D_MODEL), lambda i: (i, 0)),
        out_shape=jax.ShapeDtypeStruct((T, D_MODEL), F32),
        scratch_shapes=[pltpu.VMEM((tm + 2 * SUBLANE, D_MODEL), F32)],
        compiler_params=pltpu.CompilerParams(
            dimension_semantics=("parallel",), vmem_limit_bytes=VMEM_LIMIT_BYTES),
        name="gated_conv",
    )(x2d, x2d, x2d, gain, w_in, conv_w, w_out)


def _prep_attn_weights(w_qkv, q_gain, k_gain):
    t = np.arange(N_KV_HEADS)[:, None, None]
    g = np.arange(GQA_GROUP)[None, :, None]
    i = np.arange(HALF_DIM)[None, None, :]
    q0 = ((t * GQA_GROUP + g) * HEAD_DIM + 2 * i).reshape(-1)
    k0 = np.broadcast_to(Q_DIM + t * HEAD_DIM + 2 * i, (N_KV_HEADS, GQA_GROUP, HALF_DIM)).reshape(-1)
    cols = np.concatenate([q0, q0 + 1, k0, k0 + 1])
    w_main = jnp.take(w_qkv, jnp.asarray(cols), axis=1)
    w_v = w_qkv[:, Q_DIM + KV_DIM:].reshape(D_MODEL, N_KV_HEADS, HEAD_DIM)
    w_v = jnp.pad(w_v, ((0, 0), (0, 0), (0, LANE - HEAD_DIM))).reshape(D_MODEL, V_SLAB)
    w_vt = w_v.T.astype(BF16)
    w_big = w_main.astype(BF16)

    lane = np.arange(HALF_SLAB)
    e_q = (lane[:, None] // HALF_DIM == lane[None, :] // HALF_DIM).astype(np.float32)
    e_k = (lane[:, None] // LANE == lane[None, :] // LANE).astype(np.float32) / GQA_GROUP
    reps = HALF_SLAB // HALF_DIM
    head_gains = jnp.stack([jnp.tile(q_gain[0::2], reps), jnp.tile(q_gain[1::2], reps),
                            jnp.tile(k_gain[0::2], reps), jnp.tile(k_gain[1::2], reps)])
    return w_big, w_vt, jnp.asarray(e_q, BF16), jnp.asarray(e_k, BF16), head_gains


def _rope_tables(length):
    rows = length // GRID_W
    row_ids = jnp.repeat(jnp.arange(rows, dtype=F32), GRID_W)
    col_ids = jnp.tile(jnp.arange(GRID_W, dtype=F32), rows)
    inv_freq = ROPE_THETA ** (-jnp.arange(0, ROPE_AXIS_DIM, 2, dtype=F32) / ROPE_AXIS_DIM)
    ang = jnp.concatenate([row_ids[:, None] * inv_freq[None, :],
                           col_ids[:, None] * inv_freq[None, :]], axis=-1)
    reps = LANE // HALF_DIM
    return jnp.tile(jnp.cos(ang), (1, reps)), jnp.tile(jnp.sin(ang), (1, reps))


def _trunk(x, p):
    B, L, _ = x.shape
    T = B * L
    x2d = x.reshape(T, D_MODEL)
    cos, sin = _rope_tables(L)
    q, k, vt = _qkv_call(x2d, p["g_mix0"], p["w_big"], p["w_vt"], p["e_q"], p["e_k"],
                         p["head_gains"], cos, sin, L)
    o = _attn_call(q.reshape(B, L, -1), k.reshape(B, L, -1), vt)
    x2d = _row_call(_attn_out_mlp_kernel, "attn_out_mlp", T, [x2d, o.reshape(T, Q_DIM)],
                    [p["w_o"], p["g_mlp0"], p["w1_0"], p["w2_0"]])
    x2d = _conv_call(x2d, p["g_mix1"], p["w_in"], p["conv_w"], p["w_out"], L)
    y = _row_call(_mlp_final_kernel, "mlp_final_norm", T, [x2d],
                  [p["g_mlp1"], p["w1_1"], p["w2_1"], p["g_final"]])
    return y.reshape(B, L, D_MODEL)


def kernel(x_prompt, x_sample, norm_mix, norm_mlp, w_qkv, q_gain, k_gain, w_o, w_conv_in, conv_w,
           w_conv_out, w_mlp1, w_mlp2, norm_final):
    w_big, w_vt, e_q, e_k, head_gains = _prep_attn_weights(w_qkv[0], q_gain[0], k_gain[0])
    p = {
        "g_mix0": norm_mix[0:1], "g_mix1": norm_mix[1:2],
        "g_mlp0": norm_mlp[0:1], "g_mlp1": norm_mlp[1:2],
        "g_final": norm_final[None, :],
        "w_big": w_big, "w_vt": w_vt, "e_q": e_q, "e_k": e_k, "head_gains": head_gains,
        "w_o": w_o[0].astype(BF16),
        "w1_0": w_mlp1[0].astype(BF16), "w2_0": w_mlp2[0].astype(BF16),
        "w1_1": w_mlp1[1].astype(BF16), "w2_1": w_mlp2[1].astype(BF16),
        "w_in": w_conv_in[0].astype(BF16), "conv_w": conv_w[0],
        "w_out": w_conv_out[0].astype(BF16),
    }
    return (_trunk(x_prompt, p), _trunk(x_sample, p))
```

```python
import functools

import numpy as np
import jax
import jax.numpy as jnp
from jax import lax
from jax.experimental import pallas as pl
from jax.experimental.pallas import tpu as pltpu

D_MODEL = 1024
N_HEADS = 16
N_KV_HEADS = 4
HEAD_DIM = 64
GQA_GROUP = N_HEADS // N_KV_HEADS
HALF_DIM = HEAD_DIM // 2
KV_DIM = N_KV_HEADS * HEAD_DIM
Q_DIM = N_HEADS * HEAD_DIM
D_FF = 4 * D_MODEL
ROPE_AXIS_DIM = HEAD_DIM // 2
ROPE_THETA = 10000.0
GRID_W = 64
NORM_EPS = 1e-6
CONV_WIDTH = 3

LANE = 128
SUBLANE = 8
HALF_SLAB = N_KV_HEADS * LANE
V_SLAB = N_KV_HEADS * LANE
VMEM_LIMIT_BYTES = 56 * 1024 * 1024
Q_SCALE = HEAD_DIM ** -0.5 * float(np.log2(np.e))

ROW_TILE = 512
FF_CHUNK = 1024
ATT_TQ = 256
ATT_TKB = 4096
ATT_TK = 512
QCOL = 1024
ATT_DEPTH = 2

BF16 = jnp.bfloat16
F32 = jnp.float32


def _const_spec(shape):
    zeros = (0,) * len(shape)
    return pl.BlockSpec(shape, lambda *_: zeros, pipeline_mode=pl.Buffered(1))


def _rms(x, gain):
    return x * lax.rsqrt(jnp.mean(x * x, axis=-1, keepdims=True) + NORM_EPS) * gain


def _qkv_kernel(x_ref, g_ref, w_ref, wvt_ref, eq_ref, ek_ref, hg_ref, cos_ref, sin_ref,
                q_ref, k_ref, vt_ref):
    h = _rms(x_ref[...], g_ref[...]).astype(BF16)
    y = jnp.dot(h, w_ref[...], preferred_element_type=F32)
    cos = jnp.concatenate([cos_ref[...]] * N_KV_HEADS, axis=1)
    sin = jnp.concatenate([sin_ref[...]] * N_KV_HEADS, axis=1)

    def norm_rope(a0, a1, e_ref, g0, g1, scale, out_ref):
        sq = a0 * a0 + a1 * a1
        hi = sq.astype(BF16)
        lo = (sq - hi.astype(F32)).astype(BF16)
        ss = (jnp.dot(hi, e_ref[...], preferred_element_type=F32)
              + jnp.dot(lo, e_ref[...], preferred_element_type=F32))
        r = lax.rsqrt(ss * (1.0 / HEAD_DIM) + NORM_EPS)
        n0 = a0 * r * g0
        n1 = a1 * r * g1
        o0 = ((n0 * cos - n1 * sin) * scale).astype(BF16)
        o1 = ((n0 * sin + n1 * cos) * scale).astype(BF16)
        for t in range(N_KV_HEADS):
            out_ref[:, 2 * t * LANE:(2 * t + 1) * LANE] = o0[:, t * LANE:(t + 1) * LANE]
            out_ref[:, (2 * t + 1) * LANE:(2 * t + 2) * LANE] = o1[:, t * LANE:(t + 1) * LANE]

    S = HALF_SLAB
    norm_rope(y[:, 0:S], y[:, S:2 * S], eq_ref, hg_ref[0:1, :], hg_ref[1:2, :],
              Q_SCALE, q_ref)
    norm_rope(y[:, 2 * S:3 * S], y[:, 3 * S:4 * S], ek_ref, hg_ref[2:3, :], hg_ref[3:4, :],
              1.0, k_ref)
    yvt = lax.dot_general(wvt_ref[...], h, (((1,), (1,)), ((), ())), preferred_element_type=F32)
    vrow = lax.broadcasted_iota(jnp.int32, yvt.shape, 0)
    vt_ref[...] = jnp.where(vrow % LANE == HEAD_DIM, 1.0, yvt).astype(BF16)


def _qkv_call(x2d, gain, w_big, w_vt, e_q, e_k, head_gains, cos, sin, seq_len):
    T = x2d.shape[0]
    tm = ROW_TILE
    tiles_per_seq = seq_len // tm
    row = lambda i: (i, 0)
    pos = lambda i: (i % tiles_per_seq, 0)
    return pl.pallas_call(
        _qkv_kernel,
        grid=(T // tm,),
        in_specs=[
            pl.BlockSpec((tm, D_MODEL), row),
            _const_spec((1, D_MODEL)),
            _const_spec((D_MODEL, 4 * HALF_SLAB)),
            _const_spec((V_SLAB, D_MODEL)),
            _const_spec((HALF_SLAB, HALF_SLAB)),
            _const_spec((HALF_SLAB, HALF_SLAB)),
            _const_spec((4, HALF_SLAB)),
            pl.BlockSpec((tm, LANE), pos),
            pl.BlockSpec((tm, LANE), pos),
        ],
        out_specs=[
            pl.BlockSpec((tm, 2 * HALF_SLAB), row),
            pl.BlockSpec((tm, 2 * HALF_SLAB), row),
            pl.BlockSpec((V_SLAB, tm), lambda i: (0, i)),
        ],
        out_shape=[
            jax.ShapeDtypeStruct((T, 2 * HALF_SLAB), BF16),
            jax.ShapeDtypeStruct((T, 2 * HALF_SLAB), BF16),
            jax.ShapeDtypeStruct((V_SLAB, T), BF16),
        ],
        compiler_params=pltpu.CompilerParams(
            dimension_semantics=("parallel",), vmem_limit_bytes=VMEM_LIMIT_BYTES),
        name="qkv_norm_rope",
    )(x2d, gain, w_big, w_vt, e_q, e_k, head_gains, cos, sin)


def _attn_kernel(q_ref, k_ref, vt_ref, o_ref, qm_ref, m_ref, acc_ref, *, tq, tk, n_chunks):
    j = pl.program_id(3)

    @pl.when(j == 0)
    def _():
        q = q_ref[...]
        lane = lax.broadcasted_iota(jnp.int32, q.shape, 1)
        head_of_lane = (lane % LANE) // HALF_DIM
        for g in range(GQA_GROUP):
            qm_ref[g * tq:(g + 1) * tq, :] = jnp.where(head_of_lane == g, q, jnp.zeros_like(q))
        m_ref[...] = jnp.full(m_ref.shape, -jnp.inf, F32)
        acc_ref[...] = jnp.zeros(acc_ref.shape, F32)

    n_col = GQA_GROUP * tq // QCOL
    units = [(c, n) for c in range(n_chunks) for n in range(n_col)]

    def scores(c, n):
        kc = k_ref[c * tk:(c + 1) * tk, :]
        qn = qm_ref[n * QCOL:(n + 1) * QCOL, :]
        return lax.dot_general(kc, qn, (((1,), (1,)), ((), ())),
                               preferred_element_type=F32)

    def softmax_pv(s, c, n):
        cols = slice(n * QCOL, (n + 1) * QCOL)
        vtc = vt_ref[:, c * tk:(c + 1) * tk]
        m_prev = m_ref[:, cols]
        m_new = jnp.maximum(m_prev, jnp.max(s, axis=0, keepdims=True))
        alpha = jnp.exp2(m_prev - m_new)
        p = jnp.exp2(s - m_new).astype(BF16)
        acc_ref[:, cols] = alpha * acc_ref[:, cols] + jnp.dot(vtc, p, preferred_element_type=F32)
        m_ref[:, cols] = m_new

    pending = [scores(*units[u]) for u in range(min(ATT_DEPTH, len(units)))]
    for u, (c, n) in enumerate(units):
        if u + ATT_DEPTH < len(units):
            pending.append(scores(*units[u + ATT_DEPTH]))
        softmax_pv(pending.pop(0), c, n)

    @pl.when(j == pl.num_programs(3) - 1)
    def _():
        acc = acc_ref[...]
        o = (acc / acc[HEAD_DIM:HEAD_DIM + 1, :]).T
        lane = lax.broadcasted_iota(jnp.int32, (tq, LANE), 1)
        pairs = []
        for g in range(0, GQA_GROUP, 2):
            lo = o[g * tq:(g + 1) * tq, :]
            hi = pltpu.roll(o[(g + 1) * tq:(g + 2) * tq, :], HEAD_DIM, 1)
            pairs.append(jnp.where(lane < HEAD_DIM, lo, hi))
        o_ref[...] = jnp.concatenate(pairs, axis=1).astype(BF16)


def _attn_call(q, k, vt):
    B, L, _ = q.shape
    tq = min(ATT_TQ, L)
    tkb = min(ATT_TKB, L)
    tk = min(ATT_TK, tkb)
    kv_blocks = L // tkb
    kern = functools.partial(_attn_kernel, tq=tq, tk=tk, n_chunks=tkb // tk)
    return pl.pallas_call(
        kern,
        grid=(B, N_KV_HEADS, L // tq, kv_blocks),
        in_specs=[
            pl.BlockSpec((None, tq, 2 * LANE), lambda b, t, i, j: (b, i, t)),
            pl.BlockSpec((None, tkb, 2 * LANE), lambda b, t, i, j: (b, j, t)),
            pl.BlockSpec((LANE, tkb), lambda b, t, i, j: (t, b * kv_blocks + j)),
        ],
        out_specs=pl.BlockSpec((None, tq, GQA_GROUP * HEAD_DIM), lambda b, t, i, j: (b, i, t)),
        out_shape=jax.ShapeDtypeStruct((B, L, Q_DIM), BF16),
        scratch_shapes=[
            pltpu.VMEM((GQA_GROUP * tq, 2 * LANE), BF16),
            pltpu.VMEM((1, GQA_GROUP * tq), F32),
            pltpu.VMEM((LANE, GQA_GROUP * tq), F32),
        ],
        compiler_params=pltpu.CompilerParams(
            dimension_semantics=("parallel", "parallel", "parallel", "arbitrary"),
            vmem_limit_bytes=VMEM_LIMIT_BYTES),
        name="flash_attn",
    )(q, k, vt)


def _mlp_body(x1, gm, w1_ref, w2_ref):
    h = _rms(x1, gm).astype(BF16)
    acc = jnp.zeros_like(x1)
    for c in range(D_FF // FF_CHUNK):
        cols = slice(c * FF_CHUNK, (c + 1) * FF_CHUNK)
        hid = jnp.maximum(jnp.dot(h, w1_ref[:, cols], preferred_element_type=F32), 0.0)
        acc = acc + jnp.dot((hid * hid).astype(BF16), w2_ref[cols, :], preferred_element_type=F32)
    return x1 + acc


def _attn_out_mlp_kernel(x_ref, o_ref, wo_ref, gm_ref, w1_ref, w2_ref, out_ref):
    x1 = x_ref[...] + jnp.dot(o_ref[...], wo_ref[...], preferred_element_type=F32)
    out_ref[...] = _mlp_body(x1, gm_ref[...], w1_ref, w2_ref)


def _mlp_final_kernel(x_ref, gm_ref, w1_ref, w2_ref, gf_ref, out_ref):
    y = _mlp_body(x_ref[...], gm_ref[...], w1_ref, w2_ref)
    out_ref[...] = _rms(y, gf_ref[...])


def _row_call(kernel, name, T, row_inputs, const_inputs):
    tm = ROW_TILE
    row = lambda i: (i, 0)
    in_specs = [pl.BlockSpec((tm, a.shape[1]), row) for a in row_inputs]
    in_specs += [_const_spec(a.shape) for a in const_inputs]
    return pl.pallas_call(
        kernel,
        grid=(T // tm,),
        in_specs=in_specs,
        out_specs=pl.BlockSpec((tm, D_MODEL), row),
        out_shape=jax.ShapeDtypeStruct((T, D_MODEL), F32),
        compiler_params=pltpu.CompilerParams(
            dimension_semantics=("parallel",), vmem_limit_bytes=VMEM_LIMIT_BYTES),
        name=name,
    )(*row_inputs, *const_inputs)


def _conv_kernel(x_ref, xp_ref, xn_ref, g_ref, win_ref, cw_ref, wout_ref, out_ref, u_ref,
                 *, tm, tiles_per_seq):
    i = pl.program_id(0)
    first = (i % tiles_per_seq) == 0
    last = (i % tiles_per_seq) == tiles_per_seq - 1
    x = x_ref[...]
    xa = jnp.concatenate([xp_ref[...], x, xn_ref[...]], axis=0)
    h = _rms(xa, g_ref[...]).astype(BF16)
    bcx = jnp.dot(h, win_ref[...], preferred_element_type=F32)
    u = bcx[:, D_MODEL:2 * D_MODEL] * bcx[:, 2 * D_MODEL:]
    r = lax.broadcasted_iota(jnp.int32, (tm + 2 * SUBLANE, 1), 0)
    pad = (first & (r < SUBLANE)) | (last & (r >= tm + SUBLANE))
    u_ref[...] = jnp.where(pad, 0.0, u)
    z = (u_ref[pl.ds(SUBLANE - 1, tm), :] * cw_ref[0:1, :]
         + u_ref[pl.ds(SUBLANE, tm), :] * cw_ref[1:2, :]
         + u_ref[pl.ds(SUBLANE + 1, tm), :] * cw_ref[2:3, :])
    gated = (bcx[SUBLANE:SUBLANE + tm, :D_MODEL] * z).astype(BF16)
    out_ref[...] = x + jnp.dot(gated, wout_ref[...], preferred_element_type=F32)


def _conv_call(x2d, gain, w_in, conv_w, w_out, seq_len):
    T = x2d.shape[0]
    tm = ROW_TILE
    halo_blocks = tm // SUBLANE
    n_halo = T // SUBLANE
    kern = functools.partial(_conv_kernel, tm=tm, tiles_per_seq=seq_len // tm)
    return pl.pallas_call(
        kern,
        grid=(T // tm,),
        in_specs=[
            pl.BlockSpec((tm, D_MODEL), lambda i: (i, 0)),
            pl.BlockSpec((SUBLANE, D_MODEL), lambda i: (jnp.maximum(i * halo_blocks - 1, 0), 0)),
            pl.BlockSpec((SUBLANE, D_MODEL),
                         lambda i: (jnp.minimum((i + 1) * halo_blocks, n_halo - 1), 0)),
            _const_spec((1, D_MODEL)),
            _const_spec((D_MODEL, 3 * D_MODEL)),
            _const_spec((CONV_WIDTH, D_MODEL)),
            _const_spec((D_MODEL, D_MODEL)),
        ],
        out_specs=pl.BlockSpec((tm, D_MODEL), lambda i: (i, 0)),
        out_shape=jax.ShapeDtypeStruct((T, D_MODEL), F32),
        scratch_shapes=[pltpu.VMEM((tm + 2 * SUBLANE, D_MODEL), F32)],
        compiler_params=pltpu.CompilerParams(
            dimension_semantics=("parallel",), vmem_limit_bytes=VMEM_LIMIT_BYTES),
        name="gated_conv",
    )(x2d, x2d, x2d, gain, w_in, conv_w, w_out)


def _prep_attn_weights(w_qkv, q_gain, k_gain):
    t = np.arange(N_KV_HEADS)[:, None, None]
    g = np.arange(GQA_GROUP)[None, :, None]
    i = np.arange(HALF_DIM)[None, None, :]
    q0 = ((t * GQA_GROUP + g) * HEAD_DIM + 2 * i).reshape(-1)
    k0 = np.broadcast_to(Q_DIM + t * HEAD_DIM + 2 * i, (N_KV_HEADS, GQA_GROUP, HALF_DIM)).reshape(-1)
    cols = np.concatenate([q0, q0 + 1, k0, k0 + 1])
    w_main = jnp.take(w_qkv, jnp.asarray(cols), axis=1)
    w_v = w_qkv[:, Q_DIM + KV_DIM:].reshape(D_MODEL, N_KV_HEADS, HEAD_DIM)
    w_v = jnp.pad(w_v, ((0, 0), (0, 0), (0, LANE - HEAD_DIM))).reshape(D_MODEL, V_SLAB)
    w_vt = w_v.T.astype(BF16)
    w_big = w_main.astype(BF16)

    lane = np.arange(HALF_SLAB)
    e_q = (lane[:, None] // HALF_DIM == lane[None, :] // HALF_DIM).astype(np.float32)
    e_k = (lane[:, None] // LANE == lane[None, :] // LANE).astype(np.float32) / GQA_GROUP
    reps = HALF_SLAB // HALF_DIM
    head_gains = jnp.stack([jnp.tile(q_gain[0::2], reps), jnp.tile(q_gain[1::2], reps),
                            jnp.tile(k_gain[0::2], reps), jnp.tile(k_gain[1::2], reps)])
    return w_big, w_vt, jnp.asarray(e_q, BF16), jnp.asarray(e_k, BF16), head_gains


def _rope_tables(length):
    rows = length // GRID_W
    row_ids = jnp.repeat(jnp.arange(rows, dtype=F32), GRID_W)
    col_ids = jnp.tile(jnp.arange(GRID_W, dtype=F32), rows)
    inv_freq = ROPE_THETA ** (-jnp.arange(0, ROPE_AXIS_DIM, 2, dtype=F32) / ROPE_AXIS_DIM)
    ang = jnp.concatenate([row_ids[:, None] * inv_freq[None, :],
                           col_ids[:, None] * inv_freq[None, :]], axis=-1)
    reps = LANE // HALF_DIM
    return jnp.tile(jnp.cos(ang), (1, reps)), jnp.tile(jnp.sin(ang), (1, reps))


def _trunk(x, p):
    B, L, _ = x.shape
    T = B * L
    x2d = x.reshape(T, D_MODEL)
    cos, sin = _rope_tables(L)
    q, k, vt = _qkv_call(x2d, p["g_mix0"], p["w_big"], p["w_vt"], p["e_q"], p["e_k"],
                         p["head_gains"], cos, sin, L)
    o = _attn_call(q.reshape(B, L, -1), k.reshape(B, L, -1), vt)
    x2d = _row_call(_attn_out_mlp_kernel, "attn_out_mlp", T, [x2d, o.reshape(T, Q_DIM)],
                    [p["w_o"], p["g_mlp0"], p["w1_0"], p["w2_0"]])
    x2d = _conv_call(x2d, p["g_mix1"], p["w_in"], p["conv_w"], p["w_out"], L)
    y = _row_call(_mlp_final_kernel, "mlp_final_norm", T, [x2d],
                  [p["g_mlp1"], p["w1_1"], p["w2_1"], p["g_final"]])
    return y.reshape(B, L, D_MODEL)


def kernel(x_prompt, x_sample, norm_mix, norm_mlp, w_qkv, q_gain, k_gain, w_o, w_conv_in, conv_w,
           w_conv_out, w_mlp1, w_mlp2, norm_final):
    w_big, w_vt, e_q, e_k, head_gains = _prep_attn_weights(w_qkv[0], q_gain[0], k_gain[0])
    p = {
        "g_mix0": norm_mix[0:1], "g_mix1": norm_mix[1:2],
        "g_mlp0": norm_mlp[0:1], "g_mlp1": norm_mlp[1:2],
        "g_final": norm_final[None, :],
        "w_big": w_big, "w_vt": w_vt, "e_q": e_q, "e_k": e_k, "head_gains": head_gains,
        "w_o": w_o[0].astype(BF16),
        "w1_0": w_mlp1[0].astype(BF16), "w2_0": w_mlp2[0].astype(BF16),
        "w1_1": w_mlp1[1].astype(BF16), "w2_1": w_mlp2[1].astype(BF16),
        "w_in": w_conv_in[0].astype(BF16), "conv_w": conv_w[0],
        "w_out": w_conv_out[0].astype(BF16),
    }
    return (_trunk(x_prompt, p), _trunk(x_sample, p))
```

```python
import functools

import numpy as np
import jax
import jax.numpy as jnp
from jax import lax
from jax.experimental import pallas as pl
from jax.experimental.pallas import tpu as pltpu

D_MODEL = 1024
N_HEADS = 16
N_KV_HEADS = 4
HEAD_DIM = 64
GQA_GROUP = N_HEADS // N_KV_HEADS
HALF_DIM = HEAD_DIM // 2
KV_DIM = N_KV_HEADS * HEAD_DIM
Q_DIM = N_HEADS * HEAD_DIM
D_FF = 4 * D_MODEL
ROPE_AXIS_DIM = HEAD_DIM // 2
ROPE_THETA = 10000.0
GRID_W = 64
NORM_EPS = 1e-6
CONV_WIDTH = 3

LANE = 128
SUBLANE = 8
HALF_SLAB = N_KV_HEADS * LANE
V_SLAB = N_KV_HEADS * LANE
VMEM_LIMIT_BYTES = 56 * 1024 * 1024
Q_SCALE = HEAD_DIM ** -0.5 * float(np.log2(np.e))
SCORE_BOUND_LOG2 = 60.0
F32_MAX_LOG2 = 126.0
BF16_ROUNDING_MARGIN = (1.0 + 2.0 ** -8) ** 2

ROW_TILE = 512
FF_CHUNK = 1024
ATT_TQ = 256
ATT_TKB = 4096
ATT_TK = 512
QCOL = 1024
ATT_DEPTH = 2

BF16 = jnp.bfloat16
F32 = jnp.float32


def _const_spec(shape):
    zeros = (0,) * len(shape)
    return pl.BlockSpec(shape, lambda *_: zeros, pipeline_mode=pl.Buffered(1))


def _rms(x, gain):
    return x * lax.rsqrt(jnp.mean(x * x, axis=-1, keepdims=True) + NORM_EPS) * gain


def _qkv_kernel(x_ref, g_ref, w_ref, wvt_ref, eq_ref, ek_ref, hg_ref, cos_ref, sin_ref,
                q_ref, k_ref, vt_ref):
    h = _rms(x_ref[...], g_ref[...]).astype(BF16)
    y = jnp.dot(h, w_ref[...], preferred_element_type=F32)
    cos = jnp.concatenate([cos_ref[...]] * N_KV_HEADS, axis=1)
    sin = jnp.concatenate([sin_ref[...]] * N_KV_HEADS, axis=1)

    def norm_rope(a0, a1, e_ref, g0, g1, scale, out_ref):
        sq = a0 * a0 + a1 * a1
        hi = sq.astype(BF16)
        lo = (sq - hi.astype(F32)).astype(BF16)
        ss = (jnp.dot(hi, e_ref[...], preferred_element_type=F32)
              + jnp.dot(lo, e_ref[...], preferred_element_type=F32))
        r = lax.rsqrt(ss * (1.0 / HEAD_DIM) + NORM_EPS)
        n0 = a0 * r * g0
        n1 = a1 * r * g1
        o0 = ((n0 * cos - n1 * sin) * scale).astype(BF16)
        o1 = ((n0 * sin + n1 * cos) * scale).astype(BF16)
        for t in range(N_KV_HEADS):
            out_ref[:, 2 * t * LANE:(2 * t + 1) * LANE] = o0[:, t * LANE:(t + 1) * LANE]
            out_ref[:, (2 * t + 1) * LANE:(2 * t + 2) * LANE] = o1[:, t * LANE:(t + 1) * LANE]

    S = HALF_SLAB
    norm_rope(y[:, 0:S], y[:, S:2 * S], eq_ref, hg_ref[0:1, :], hg_ref[1:2, :],
              Q_SCALE, q_ref)
    norm_rope(y[:, 2 * S:3 * S], y[:, 3 * S:4 * S], ek_ref, hg_ref[2:3, :], hg_ref[3:4, :],
              1.0, k_ref)
    yvt = lax.dot_general(wvt_ref[...], h, (((1,), (1,)), ((), ())), preferred_element_type=F32)
    vrow = lax.broadcasted_iota(jnp.int32, yvt.shape, 0)
    vt_ref[...] = jnp.where(vrow % LANE == HEAD_DIM, 1.0, yvt).astype(BF16)


def _qkv_call(x2d, gain, w_big, w_vt, e_q, e_k, head_gains, cos, sin, seq_len):
    T = x2d.shape[0]
    tm = ROW_TILE
    tiles_per_seq = seq_len // tm
    row = lambda i: (i, 0)
    pos = lambda i: (i % tiles_per_seq, 0)
    return pl.pallas_call(
        _qkv_kernel,
        grid=(T // tm,),
        in_specs=[
            pl.BlockSpec((tm, D_MODEL), row),
            _const_spec((1, D_MODEL)),
            _const_spec((D_MODEL, 4 * HALF_SLAB)),
            _const_spec((V_SLAB, D_MODEL)),
            _const_spec((HALF_SLAB, HALF_SLAB)),
            _const_spec((HALF_SLAB, HALF_SLAB)),
            _const_spec((4, HALF_SLAB)),
            pl.BlockSpec((tm, LANE), pos),
            pl.BlockSpec((tm, LANE), pos),
        ],
        out_specs=[
            pl.BlockSpec((tm, 2 * HALF_SLAB), row),
            pl.BlockSpec((tm, 2 * HALF_SLAB), row),
            pl.BlockSpec((V_SLAB, tm), lambda i: (0, i)),
        ],
        out_shape=[
            jax.ShapeDtypeStruct((T, 2 * HALF_SLAB), BF16),
            jax.ShapeDtypeStruct((T, 2 * HALF_SLAB), BF16),
            jax.ShapeDtypeStruct((V_SLAB, T), BF16),
        ],
        compiler_params=pltpu.CompilerParams(
            dimension_semantics=("parallel",), vmem_limit_bytes=VMEM_LIMIT_BYTES),
        name="qkv_norm_rope",
    )(x2d, gain, w_big, w_vt, e_q, e_k, head_gains, cos, sin)


def _attn_kernel(q_ref, k_ref, vt_ref, o_ref, qm_ref, m_ref, acc_ref, *, tq, tk, n_chunks,
                 running_max):
    j = pl.program_id(3)

    @pl.when(j == 0)
    def _():
        q = q_ref[...]
        lane = lax.broadcasted_iota(jnp.int32, q.shape, 1)
        head_of_lane = (lane % LANE) // HALF_DIM
        for g in range(GQA_GROUP):
            qm_ref[g * tq:(g + 1) * tq, :] = jnp.where(head_of_lane == g, q, jnp.zeros_like(q))
        m_ref[...] = jnp.full(m_ref.shape, -jnp.inf, F32)
        acc_ref[...] = jnp.zeros(acc_ref.shape, F32)

    n_col = GQA_GROUP * tq // QCOL
    units = [(c, n) for c in range(n_chunks) for n in range(n_col)]

    def scores(c, n):
        kc = k_ref[c * tk:(c + 1) * tk, :]
        qn = qm_ref[n * QCOL:(n + 1) * QCOL, :]
        return lax.dot_general(kc, qn, (((1,), (1,)), ((), ())),
                               preferred_element_type=F32)

    def softmax_pv(s, c, n):
        cols = slice(n * QCOL, (n + 1) * QCOL)
        vtc = vt_ref[:, c * tk:(c + 1) * tk]
        if not running_max:
            p = jnp.exp2(s).astype(BF16)
            acc_ref[:, cols] += jnp.dot(vtc, p, preferred_element_type=F32)
            return
        m_prev = m_ref[:, cols]
        m_new = jnp.maximum(m_prev, jnp.max(s, axis=0, keepdims=True))
        alpha = jnp.exp2(m_prev - m_new)
        p = jnp.exp2(s - m_new).astype(BF16)
        acc_ref[:, cols] = alpha * acc_ref[:, cols] + jnp.dot(vtc, p, preferred_element_type=F32)
        m_ref[:, cols] = m_new

    pending = [scores(*units[u]) for u in range(min(ATT_DEPTH, len(units)))]
    for u, (c, n) in enumerate(units):
        if u + ATT_DEPTH < len(units):
            pending.append(scores(*units[u + ATT_DEPTH]))
        softmax_pv(pending.pop(0), c, n)

    @pl.when(j == pl.num_programs(3) - 1)
    def _():
        acc = acc_ref[...]
        o = (acc / acc[HEAD_DIM:HEAD_DIM + 1, :]).T
        lane = lax.broadcasted_iota(jnp.int32, (tq, LANE), 1)
        pairs = []
        for g in range(0, GQA_GROUP, 2):
            lo = o[g * tq:(g + 1) * tq, :]
            hi = pltpu.roll(o[(g + 1) * tq:(g + 2) * tq, :], HEAD_DIM, 1)
            pairs.append(jnp.where(lane < HEAD_DIM, lo, hi))
        o_ref[...] = jnp.concatenate(pairs, axis=1).astype(BF16)


def _attn_call(q, k, vt, running_max):
    B, L, _ = q.shape
    tq = min(ATT_TQ, L)
    tkb = min(ATT_TKB, L)
    tk = min(ATT_TK, tkb)
    kv_blocks = L // tkb
    kern = functools.partial(_attn_kernel, tq=tq, tk=tk, n_chunks=tkb // tk,
                             running_max=running_max)
    return pl.pallas_call(
        kern,
        grid=(B, N_KV_HEADS, L // tq, kv_blocks),
        in_specs=[
            pl.BlockSpec((None, tq, 2 * LANE), lambda b, t, i, j: (b, i, t)),
            pl.BlockSpec((None, tkb, 2 * LANE), lambda b, t, i, j: (b, j, t)),
            pl.BlockSpec((LANE, tkb), lambda b, t, i, j: (t, b * kv_blocks + j)),
        ],
        out_specs=pl.BlockSpec((None, tq, GQA_GROUP * HEAD_DIM), lambda b, t, i, j: (b, i, t)),
        out_shape=jax.ShapeDtypeStruct((B, L, Q_DIM), BF16),
        scratch_shapes=[
            pltpu.VMEM((GQA_GROUP * tq, 2 * LANE), BF16),
            pltpu.VMEM((1, GQA_GROUP * tq), F32),
            pltpu.VMEM((LANE, GQA_GROUP * tq), F32),
        ],
        compiler_params=pltpu.CompilerParams(
            dimension_semantics=("parallel", "parallel", "parallel", "arbitrary"),
            vmem_limit_bytes=VMEM_LIMIT_BYTES),
        name="flash_attn" if running_max else "flash_attn_bounded",
    )(q, k, vt)


def _mlp_body(x1, gm, w1_ref, w2_ref):
    h = _rms(x1, gm).astype(BF16)
    acc = jnp.zeros_like(x1)
    for c in range(D_FF // FF_CHUNK):
        cols = slice(c * FF_CHUNK, (c + 1) * FF_CHUNK)
        hid = jnp.maximum(jnp.dot(h, w1_ref[:, cols], preferred_element_type=F32), 0.0)
        acc = acc + jnp.dot((hid * hid).astype(BF16), w2_ref[cols, :], preferred_element_type=F32)
    return x1 + acc


def _attn_out_mlp_kernel(x_ref, o_ref, wo_ref, gm_ref, w1_ref, w2_ref, out_ref):
    x1 = x_ref[...] + jnp.dot(o_ref[...], wo_ref[...], preferred_element_type=F32)
    out_ref[...] = _mlp_body(x1, gm_ref[...], w1_ref, w2_ref)


def _mlp_final_kernel(x_ref, gm_ref, w1_ref, w2_ref, gf_ref, out_ref):
    y = _mlp_body(x_ref[...], gm_ref[...], w1_ref, w2_ref)
    out_ref[...] = _rms(y, gf_ref[...])


def _row_call(kernel, name, T, row_inputs, const_inputs):
    tm = ROW_TILE
    row = lambda i: (i, 0)
    in_specs = [pl.BlockSpec((tm, a.shape[1]), row) for a in row_inputs]
    in_specs += [_const_spec(a.shape) for a in const_inputs]
    return pl.pallas_call(
        kernel,
        grid=(T // tm,),
        in_specs=in_specs,
        out_specs=pl.BlockSpec((tm, D_MODEL), row),
        out_shape=jax.ShapeDtypeStruct((T, D_MODEL), F32),
        compiler_params=pltpu.CompilerParams(
            dimension_semantics=("parallel",), vmem_limit_bytes=VMEM_LIMIT_BYTES),
        name=name,
    )(*row_inputs, *const_inputs)


def _conv_kernel(x_ref, xp_ref, xn_ref, g_ref, win_ref, cw_ref, wout_ref, out_ref, u_ref,
                 *, tm, tiles_per_seq):
    i = pl.program_id(0)
    first = (i % tiles_per_seq) == 0
    last = (i % tiles_per_seq) == tiles_per_seq - 1
    x = x_ref[...]
    xa = jnp.concatenate([xp_ref[...], x, xn_ref[...]], axis=0)
    h = _rms(xa, g_ref[...]).astype(BF16)
    bcx = jnp.dot(h, win_ref[...], preferred_element_type=F32)
    u = bcx[:, D_MODEL:2 * D_MODEL] * bcx[:, 2 * D_MODEL:]
    r = lax.broadcasted_iota(jnp.int32, (tm + 2 * SUBLANE, 1), 0)
    pad = (first & (r < SUBLANE)) | (last & (r >= tm + SUBLANE))
    u_ref[...] = jnp.where(pad, 0.0, u)
    z = (u_ref[pl.ds(SUBLANE - 1, tm), :] * cw_ref[0:1, :]
         + u_ref[pl.ds(SUBLANE, tm), :] * cw_ref[1:2, :]
         + u_ref[pl.ds(SUBLANE + 1, tm), :] * cw_ref[2:3, :])
    gated = (bcx[SUBLANE:SUBLANE + tm, :D_MODEL] * z).astype(BF16)
    out_ref[...] = x + jnp.dot(gated, wout_ref[...], preferred_element_type=F32)


def _conv_call(x2d, gain, w_in, conv_w, w_out, seq_len):
    T = x2d.shape[0]
    tm = ROW_TILE
    halo_blocks = tm // SUBLANE
    n_halo = T // SUBLANE
    kern = functools.partial(_conv_kernel, tm=tm, tiles_per_seq=seq_len // tm)
    return pl.pallas_call(
        kern,
        grid=(T // tm,),
        in_specs=[
            pl.BlockSpec((tm, D_MODEL), lambda i: (i, 0)),
            pl.BlockSpec((SUBLANE, D_MODEL), lambda i: (jnp.maximum(i * halo_blocks - 1, 0), 0)),
            pl.BlockSpec((SUBLANE, D_MODEL),
                         lambda i: (jnp.minimum((i + 1) * halo_blocks, n_halo - 1), 0)),
            _const_spec((1, D_MODEL)),
            _const_spec((D_MODEL, 3 * D_MODEL)),
            _const_spec((CONV_WIDTH, D_MODEL)),
            _const_spec((D_MODEL, D_MODEL)),
        ],
        out_specs=pl.BlockSpec((tm, D_MODEL), lambda i: (i, 0)),
        out_shape=jax.ShapeDtypeStruct((T, D_MODEL), F32),
        scratch_shapes=[pltpu.VMEM((tm + 2 * SUBLANE, D_MODEL), F32)],
        compiler_params=pltpu.CompilerParams(
            dimension_semantics=("parallel",), vmem_limit_bytes=VMEM_LIMIT_BYTES),
        name="gated_conv",
    )(x2d, x2d, x2d, gain, w_in, conv_w, w_out)


def _prep_attn_weights(w_qkv, q_gain, k_gain):
    t = np.arange(N_KV_HEADS)[:, None, None]
    g = np.arange(GQA_GROUP)[None, :, None]
    i = np.arange(HALF_DIM)[None, None, :]
    q0 = ((t * GQA_GROUP + g) * HEAD_DIM + 2 * i).reshape(-1)
    k0 = np.broadcast_to(Q_DIM + t * HEAD_DIM + 2 * i, (N_KV_HEADS, GQA_GROUP, HALF_DIM)).reshape(-1)
    cols = np.concatenate([q0, q0 + 1, k0, k0 + 1])
    w_main = jnp.take(w_qkv, jnp.asarray(cols), axis=1)
    w_v = w_qkv[:, Q_DIM + KV_DIM:].reshape(D_MODEL, N_KV_HEADS, HEAD_DIM)
    w_v = jnp.pad(w_v, ((0, 0), (0, 0), (0, LANE - HEAD_DIM))).reshape(D_MODEL, V_SLAB)
    w_vt = w_v.T.astype(BF16)
    w_big = w_main.astype(BF16)

    lane = np.arange(HALF_SLAB)
    e_q = (lane[:, None] // HALF_DIM == lane[None, :] // HALF_DIM).astype(np.float32)
    e_k = (lane[:, None] // LANE == lane[None, :] // LANE).astype(np.float32) / GQA_GROUP
    reps = HALF_SLAB // HALF_DIM
    head_gains = jnp.stack([jnp.tile(q_gain[0::2], reps), jnp.tile(q_gain[1::2], reps),
                            jnp.tile(k_gain[0::2], reps), jnp.tile(k_gain[1::2], reps)])
    return w_big, w_vt, jnp.asarray(e_q, BF16), jnp.asarray(e_k, BF16), head_gains


def _rope_tables(length):
    rows = length // GRID_W
    row_ids = jnp.repeat(jnp.arange(rows, dtype=F32), GRID_W)
    col_ids = jnp.tile(jnp.arange(GRID_W, dtype=F32), rows)
    inv_freq = ROPE_THETA ** (-jnp.arange(0, ROPE_AXIS_DIM, 2, dtype=F32) / ROPE_AXIS_DIM)
    ang = jnp.concatenate([row_ids[:, None] * inv_freq[None, :],
                           col_ids[:, None] * inv_freq[None, :]], axis=-1)
    reps = LANE // HALF_DIM
    return jnp.tile(jnp.cos(ang), (1, reps)), jnp.tile(jnp.sin(ang), (1, reps))


def _trunk(x, p):
    B, L, _ = x.shape
    T = B * L
    x2d = x.reshape(T, D_MODEL)
    cos, sin = _rope_tables(L)
    q, k, vt = _qkv_call(x2d, p["g_mix0"], p["w_big"], p["w_vt"], p["e_q"], p["e_k"],
                         p["head_gains"], cos, sin, L)
    q3, k3 = q.reshape(B, L, -1), k.reshape(B, L, -1)
    bounded = ((p["score_bound_log2"] <= SCORE_BOUND_LOG2)
               & (p["v_bound_log2"] <= F32_MAX_LOG2 - SCORE_BOUND_LOG2 - float(np.log2(L)) - 1.0))
    o = lax.cond(bounded,
                 lambda: _attn_call(q3, k3, vt, running_max=False),
                 lambda: _attn_call(q3, k3, vt, running_max=True))
    x2d = _row_call(_attn_out_mlp_kernel, "attn_out_mlp", T, [x2d, o.reshape(T, Q_DIM)],
                    [p["w_o"], p["g_mlp0"], p["w1_0"], p["w2_0"]])
    x2d = _conv_call(x2d, p["g_mix1"], p["w_in"], p["conv_w"], p["w_out"], L)
    y = _row_call(_mlp_final_kernel, "mlp_final_norm", T, [x2d],
                  [p["g_mlp1"], p["w1_1"], p["w2_1"], p["g_final"]])
    return y.reshape(B, L, D_MODEL)


def kernel(x_prompt, x_sample, norm_mix, norm_mlp, w_qkv, q_gain, k_gain, w_o, w_conv_in, conv_w,
           w_conv_out, w_mlp1, w_mlp2, norm_final):
    w_big, w_vt, e_q, e_k, head_gains = _prep_attn_weights(w_qkv[0], q_gain[0], k_gain[0])
    score_bound_log2 = (HEAD_DIM * Q_SCALE * BF16_ROUNDING_MARGIN
                        * jnp.max(jnp.abs(q_gain[0])) * jnp.max(jnp.abs(k_gain[0])))
    w_v_col_norm = jnp.sqrt(jnp.max(jnp.sum(jnp.square(w_qkv[0][:, Q_DIM + KV_DIM:]), axis=0)))
    v_bound_log2 = jnp.log2(D_MODEL ** 0.5 * BF16_ROUNDING_MARGIN
                            * jnp.max(jnp.abs(norm_mix[0])) * w_v_col_norm)
    p = {
        "g_mix0": norm_mix[0:1], "g_mix1": norm_mix[1:2],
        "g_mlp0": norm_mlp[0:1], "g_mlp1": norm_mlp[1:2],
        "g_final": norm_final[None, :],
        "score_bound_log2": score_bound_log2, "v_bound_log2": v_bound_log2,
        "w_big": w_big, "w_vt": w_vt, "e_q": e_q, "e_k": e_k, "head_gains": head_gains,
        "w_o": w_o[0].astype(BF16),
        "w1_0": w_mlp1[0].astype(BF16), "w2_0": w_mlp2[0].astype(BF16),
        "w1_1": w_mlp1[1].astype(BF16), "w2_1": w_mlp2[1].astype(BF16),
        "w_in": w_conv_in[0].astype(BF16), "conv_w": conv_w[0],
        "w_out": w_conv_out[0].astype(BF16),
    }
    return (_trunk(x_prompt, p), _trunk(x_sample, p))
```

```python
import functools

import numpy as np
import jax
import jax.numpy as jnp
from jax import lax
from jax.experimental import pallas as pl
from jax.experimental.pallas import tpu as pltpu

D_MODEL = 1024
N_HEADS = 16
N_KV_HEADS = 4
HEAD_DIM = 64
GQA_GROUP = N_HEADS // N_KV_HEADS
HALF_DIM = HEAD_DIM // 2
KV_DIM = N_KV_HEADS * HEAD_DIM
Q_DIM = N_HEADS * HEAD_DIM
D_FF = 4 * D_MODEL
ROPE_AXIS_DIM = HEAD_DIM // 2
ROPE_THETA = 10000.0
GRID_W = 64
NORM_EPS = 1e-6
CONV_WIDTH = 3

LANE = 128
SUBLANE = 8
HALF_SLAB = N_KV_HEADS * LANE
V_SLAB = N_KV_HEADS * LANE
VMEM_LIMIT_BYTES = 56 * 1024 * 1024
Q_SCALE = HEAD_DIM ** -0.5 * float(np.log2(np.e))
SCORE_BOUND_LOG2 = 60.0
F32_MAX_LOG2 = 126.0
BF16_ROUNDING_MARGIN = (1.0 + 2.0 ** -8) ** 2

ROW_TILE = 512
FF_CHUNK = 1024
ATT_TKB = 8192
ATT_STEP_QK = 256 * 8192
ATT_CHUNK = {True: (512, 2), False: (2048, 1)}
QCOL = 1024

BF16 = jnp.bfloat16
F32 = jnp.float32


def _const_spec(shape):
    zeros = (0,) * len(shape)
    return pl.BlockSpec(shape, lambda *_: zeros, pipeline_mode=pl.Buffered(1))


def _rms(x, gain):
    return x * lax.rsqrt(jnp.mean(x * x, axis=-1, keepdims=True) + NORM_EPS) * gain


def _qkv_kernel(x_ref, g_ref, w_ref, wvt_ref, eq_ref, ek_ref, hg_ref, cos_ref, sin_ref,
                q_ref, k_ref, vt_ref):
    h = _rms(x_ref[...], g_ref[...]).astype(BF16)
    y = jnp.dot(h, w_ref[...], preferred_element_type=F32)
    cos = jnp.concatenate([cos_ref[...]] * N_KV_HEADS, axis=1)
    sin = jnp.concatenate([sin_ref[...]] * N_KV_HEADS, axis=1)

    def norm_rope(a0, a1, e_ref, g0, g1, scale, out_ref):
        sq = a0 * a0 + a1 * a1
        hi = sq.astype(BF16)
        lo = (sq - hi.astype(F32)).astype(BF16)
        ss = (jnp.dot(hi, e_ref[...], preferred_element_type=F32)
              + jnp.dot(lo, e_ref[...], preferred_element_type=F32))
        r = lax.rsqrt(ss * (1.0 / HEAD_DIM) + NORM_EPS)
        n0 = a0 * r * g0
        n1 = a1 * r * g1
        o0 = ((n0 * cos - n1 * sin) * scale).astype(BF16)
        o1 = ((n0 * sin + n1 * cos) * scale).astype(BF16)
        for t in range(N_KV_HEADS):
            out_ref[:, 2 * t * LANE:(2 * t + 1) * LANE] = o0[:, t * LANE:(t + 1) * LANE]
            out_ref[:, (2 * t + 1) * LANE:(2 * t + 2) * LANE] = o1[:, t * LANE:(t + 1) * LANE]

    S = HALF_SLAB
    norm_rope(y[:, 0:S], y[:, S:2 * S], eq_ref, hg_ref[0:1, :], hg_ref[1:2, :],
              Q_SCALE, q_ref)
    norm_rope(y[:, 2 * S:3 * S], y[:, 3 * S:4 * S], ek_ref, hg_ref[2:3, :], hg_ref[3:4, :],
              1.0, k_ref)
    yvt = lax.dot_general(wvt_ref[...], h, (((1,), (1,)), ((), ())), preferred_element_type=F32)
    vrow = lax.broadcasted_iota(jnp.int32, yvt.shape, 0)
    vt_ref[...] = jnp.where(vrow % LANE == HEAD_DIM, 1.0, yvt).astype(BF16)


def _qkv_call(x2d, gain, w_big, w_vt, e_q, e_k, head_gains, cos, sin, seq_len):
    T = x2d.shape[0]
    tm = ROW_TILE
    tiles_per_seq = seq_len // tm
    row = lambda i: (i, 0)
    pos = lambda i: (i % tiles_per_seq, 0)
    return pl.pallas_call(
        _qkv_kernel,
        grid=(T // tm,),
        in_specs=[
            pl.BlockSpec((tm, D_MODEL), row),
            _const_spec((1, D_MODEL)),
            _const_spec((D_MODEL, 4 * HALF_SLAB)),
            _const_spec((V_SLAB, D_MODEL)),
            _const_spec((HALF_SLAB, HALF_SLAB)),
            _const_spec((HALF_SLAB, HALF_SLAB)),
            _const_spec((4, HALF_SLAB)),
            pl.BlockSpec((tm, LANE), pos),
            pl.BlockSpec((tm, LANE), pos),
        ],
        out_specs=[
            pl.BlockSpec((tm, 2 * HALF_SLAB), row),
            pl.BlockSpec((tm, 2 * HALF_SLAB), row),
            pl.BlockSpec((V_SLAB, tm), lambda i: (0, i)),
        ],
        out_shape=[
            jax.ShapeDtypeStruct((T, 2 * HALF_SLAB), BF16),
            jax.ShapeDtypeStruct((T, 2 * HALF_SLAB), BF16),
            jax.ShapeDtypeStruct((V_SLAB, T), BF16),
        ],
        compiler_params=pltpu.CompilerParams(
            dimension_semantics=("parallel",), vmem_limit_bytes=VMEM_LIMIT_BYTES),
        name="qkv_norm_rope",
    )(x2d, gain, w_big, w_vt, e_q, e_k, head_gains, cos, sin)


def _attn_kernel(q_ref, k_ref, vt_ref, o_ref, qm_ref, m_ref, acc_ref, *, tq, tk, n_chunks, depth,
                 running_max):
    j = pl.program_id(3)

    @pl.when(j == 0)
    def _():
        q = q_ref[...]
        lane = lax.broadcasted_iota(jnp.int32, q.shape, 1)
        head_of_lane = (lane % LANE) // HALF_DIM
        for g in range(GQA_GROUP):
            qm_ref[g * tq:(g + 1) * tq, :] = jnp.where(head_of_lane == g, q, jnp.zeros_like(q))
        m_ref[...] = jnp.full(m_ref.shape, -jnp.inf, F32)
        acc_ref[...] = jnp.zeros(acc_ref.shape, F32)

    n_col = GQA_GROUP * tq // QCOL
    units = [(c, n) for c in range(n_chunks) for n in range(n_col)]

    def scores(c, n):
        kc = k_ref[c * tk:(c + 1) * tk, :]
        qn = qm_ref[n * QCOL:(n + 1) * QCOL, :]
        return lax.dot_general(kc, qn, (((1,), (1,)), ((), ())),
                               preferred_element_type=F32)

    def softmax_pv(s, c, n):
        cols = slice(n * QCOL, (n + 1) * QCOL)
        vtc = vt_ref[:, c * tk:(c + 1) * tk]
        if not running_max:
            p = jnp.exp2(s).astype(BF16)
            acc_ref[:, cols] += jnp.dot(vtc, p, preferred_element_type=F32)
            return
        m_prev = m_ref[:, cols]
        m_new = jnp.maximum(m_prev, jnp.max(s, axis=0, keepdims=True))
        alpha = jnp.exp2(m_prev - m_new)
        p = jnp.exp2(s - m_new).astype(BF16)
        acc_ref[:, cols] = alpha * acc_ref[:, cols] + jnp.dot(vtc, p, preferred_element_type=F32)
        m_ref[:, cols] = m_new

    pending = [scores(*units[u]) for u in range(min(depth, len(units)))]
    for u, (c, n) in enumerate(units):
        if u + depth < len(units):
            pending.append(scores(*units[u + depth]))
        softmax_pv(pending.pop(0), c, n)

    @pl.when(j == pl.num_programs(3) - 1)
    def _():
        acc = acc_ref[...]
        o = (acc / acc[HEAD_DIM:HEAD_DIM + 1, :]).T
        lane = lax.broadcasted_iota(jnp.int32, (tq, LANE), 1)
        pairs = []
        for g in range(0, GQA_GROUP, 2):
            lo = o[g * tq:(g + 1) * tq, :]
            hi = pltpu.roll(o[(g + 1) * tq:(g + 2) * tq, :], HEAD_DIM, 1)
            pairs.append(jnp.where(lane < HEAD_DIM, lo, hi))
        o_ref[...] = jnp.concatenate(pairs, axis=1).astype(BF16)


def _attn_call(q, k, vt, running_max):
    B, L, _ = q.shape
    tkb = min(ATT_TKB, L)
    tq = min(ATT_STEP_QK // tkb, L)
    tk, depth = ATT_CHUNK[running_max]
    tk = min(tk, tkb)
    kv_blocks = L // tkb
    kern = functools.partial(_attn_kernel, tq=tq, tk=tk, n_chunks=tkb // tk, depth=depth,
                             running_max=running_max)
    return pl.pallas_call(
        kern,
        grid=(B, N_KV_HEADS, L // tq, kv_blocks),
        in_specs=[
            pl.BlockSpec((None, tq, 2 * LANE), lambda b, t, i, j: (b, i, t)),
            pl.BlockSpec((None, tkb, 2 * LANE), lambda b, t, i, j: (b, j, t)),
            pl.BlockSpec((LANE, tkb), lambda b, t, i, j: (t, b * kv_blocks + j)),
        ],
        out_specs=pl.BlockSpec((None, tq, GQA_GROUP * HEAD_DIM), lambda b, t, i, j: (b, i, t)),
        out_shape=jax.ShapeDtypeStruct((B, L, Q_DIM), BF16),
        scratch_shapes=[
            pltpu.VMEM((GQA_GROUP * tq, 2 * LANE), BF16),
            pltpu.VMEM((1, GQA_GROUP * tq), F32),
            pltpu.VMEM((LANE, GQA_GROUP * tq), F32),
        ],
        compiler_params=pltpu.CompilerParams(
            dimension_semantics=("parallel", "parallel", "parallel", "arbitrary"),
            vmem_limit_bytes=VMEM_LIMIT_BYTES),
        name="flash_attn" if running_max else "flash_attn_bounded",
    )(q, k, vt)


def _mlp_body(x1, gm, w1_ref, w2_ref):
    h = _rms(x1, gm).astype(BF16)
    acc = jnp.zeros_like(x1)
    for c in range(D_FF // FF_CHUNK):
        cols = slice(c * FF_CHUNK, (c + 1) * FF_CHUNK)
        hid = jnp.maximum(jnp.dot(h, w1_ref[:, cols], preferred_element_type=F32), 0.0)
        acc = acc + jnp.dot((hid * hid).astype(BF16), w2_ref[cols, :], preferred_element_type=F32)
    return x1 + acc


def _attn_out_mlp_kernel(x_ref, o_ref, wo_ref, gm_ref, w1_ref, w2_ref, out_ref):
    x1 = x_ref[...] + jnp.dot(o_ref[...], wo_ref[...], preferred_element_type=F32)
    out_ref[...] = _mlp_body(x1, gm_ref[...], w1_ref, w2_ref)


def _mlp_final_kernel(x_ref, gm_ref, w1_ref, w2_ref, gf_ref, out_ref):
    y = _mlp_body(x_ref[...], gm_ref[...], w1_ref, w2_ref)
    out_ref[...] = _rms(y, gf_ref[...])


def _row_call(kernel, name, T, row_inputs, const_inputs):
    tm = ROW_TILE
    row = lambda i: (i, 0)
    in_specs = [pl.BlockSpec((tm, a.shape[1]), row) for a in row_inputs]
    in_specs += [_const_spec(a.shape) for a in const_inputs]
    return pl.pallas_call(
        kernel,
        grid=(T // tm,),
        in_specs=in_specs,
        out_specs=pl.BlockSpec((tm, D_MODEL), row),
        out_shape=jax.ShapeDtypeStruct((T, D_MODEL), F32),
        compiler_params=pltpu.CompilerParams(
            dimension_semantics=("parallel",), vmem_limit_bytes=VMEM_LIMIT_BYTES),
        name=name,
    )(*row_inputs, *const_inputs)


def _conv_kernel(x_ref, xp_ref, xn_ref, g_ref, win_ref, cw_ref, wout_ref, out_ref, u_ref,
                 *, tm, tiles_per_seq):
    i = pl.program_id(0)
    first = (i % tiles_per_seq) == 0
    last = (i % tiles_per_seq) == tiles_per_seq - 1
    x = x_ref[...]
    xa = jnp.concatenate([xp_ref[...], x, xn_ref[...]], axis=0)
    h = _rms(xa, g_ref[...]).astype(BF16)
    bcx = jnp.dot(h, win_ref[...], preferred_element_type=F32)
    u = bcx[:, D_MODEL:2 * D_MODEL] * bcx[:, 2 * D_MODEL:]
    r = lax.broadcasted_iota(jnp.int32, (tm + 2 * SUBLANE, 1), 0)
    pad = (first & (r < SUBLANE)) | (last & (r >= tm + SUBLANE))
    u_ref[...] = jnp.where(pad, 0.0, u)
    z = (u_ref[pl.ds(SUBLANE - 1, tm), :] * cw_ref[0:1, :]
         + u_ref[pl.ds(SUBLANE, tm), :] * cw_ref[1:2, :]
         + u_ref[pl.ds(SUBLANE + 1, tm), :] * cw_ref[2:3, :])
    gated = (bcx[SUBLANE:SUBLANE + tm, :D_MODEL] * z).astype(BF16)
    out_ref[...] = x + jnp.dot(gated, wout_ref[...], preferred_element_type=F32)


def _conv_call(x2d, gain, w_in, conv_w, w_out, seq_len):
    T = x2d.shape[0]
    tm = ROW_TILE
    halo_blocks = tm // SUBLANE
    n_halo = T // SUBLANE
    kern = functools.partial(_conv_kernel, tm=tm, tiles_per_seq=seq_len // tm)
    return pl.pallas_call(
        kern,
        grid=(T // tm,),
        in_specs=[
            pl.BlockSpec((tm, D_MODEL), lambda i: (i, 0)),
            pl.BlockSpec((SUBLANE, D_MODEL), lambda i: (jnp.maximum(i * halo_blocks - 1, 0), 0)),
            pl.BlockSpec((SUBLANE, D_MODEL),
                         lambda i: (jnp.minimum((i + 1) * halo_blocks, n_halo - 1), 0)),
            _const_spec((1, D_MODEL)),
            _const_spec((D_MODEL, 3 * D_MODEL)),
            _const_spec((CONV_WIDTH, D_MODEL)),
            _const_spec((D_MODEL, D_MODEL)),
        ],
        out_specs=pl.BlockSpec((tm, D_MODEL), lambda i: (i, 0)),
        out_shape=jax.ShapeDtypeStruct((T, D_MODEL), F32),
        scratch_shapes=[pltpu.VMEM((tm + 2 * SUBLANE, D_MODEL), F32)],
        compiler_params=pltpu.CompilerParams(
            dimension_semantics=("parallel",), vmem_limit_bytes=VMEM_LIMIT_BYTES),
        name="gated_conv",
    )(x2d, x2d, x2d, gain, w_in, conv_w, w_out)


def _prep_attn_weights(w_qkv, q_gain, k_gain):
    t = np.arange(N_KV_HEADS)[:, None, None]
    g = np.arange(GQA_GROUP)[None, :, None]
    i = np.arange(HALF_DIM)[None, None, :]
    q0 = ((t * GQA_GROUP + g) * HEAD_DIM + 2 * i).reshape(-1)
    k0 = np.broadcast_to(Q_DIM + t * HEAD_DIM + 2 * i, (N_KV_HEADS, GQA_GROUP, HALF_DIM)).reshape(-1)
    cols = np.concatenate([q0, q0 + 1, k0, k0 + 1])
    w_main = jnp.take(w_qkv, jnp.asarray(cols), axis=1)
    w_v = w_qkv[:, Q_DIM + KV_DIM:].reshape(D_MODEL, N_KV_HEADS, HEAD_DIM)
    w_v = jnp.pad(w_v, ((0, 0), (0, 0), (0, LANE - HEAD_DIM))).reshape(D_MODEL, V_SLAB)
    w_vt = w_v.T.astype(BF16)
    w_big = w_main.astype(BF16)

    lane = np.arange(HALF_SLAB)
    e_q = (lane[:, None] // HALF_DIM == lane[None, :] // HALF_DIM).astype(np.float32)
    e_k = (lane[:, None] // LANE == lane[None, :] // LANE).astype(np.float32) / GQA_GROUP
    reps = HALF_SLAB // HALF_DIM
    head_gains = jnp.stack([jnp.tile(q_gain[0::2], reps), jnp.tile(q_gain[1::2], reps),
                            jnp.tile(k_gain[0::2], reps), jnp.tile(k_gain[1::2], reps)])
    return w_big, w_vt, jnp.asarray(e_q, BF16), jnp.asarray(e_k, BF16), head_gains


def _rope_tables(length):
    rows = length // GRID_W
    row_ids = jnp.repeat(jnp.arange(rows, dtype=F32), GRID_W)
    col_ids = jnp.tile(jnp.arange(GRID_W, dtype=F32), rows)
    inv_freq = ROPE_THETA ** (-jnp.arange(0, ROPE_AXIS_DIM, 2, dtype=F32) / ROPE_AXIS_DIM)
    ang = jnp.concatenate([row_ids[:, None] * inv_freq[None, :],
                           col_ids[:, None] * inv_freq[None, :]], axis=-1)
    reps = LANE // HALF_DIM
    return jnp.tile(jnp.cos(ang), (1, reps)), jnp.tile(jnp.sin(ang), (1, reps))


def _trunk(x, p):
    B, L, _ = x.shape
    T = B * L
    x2d = x.reshape(T, D_MODEL)
    cos, sin = _rope_tables(L)
    q, k, vt = _qkv_call(x2d, p["g_mix0"], p["w_big"], p["w_vt"], p["e_q"], p["e_k"],
                         p["head_gains"], cos, sin, L)
    q3, k3 = q.reshape(B, L, -1), k.reshape(B, L, -1)
    bounded = ((p["score_bound_log2"] <= SCORE_BOUND_LOG2)
               & (p["v_bound_log2"] <= F32_MAX_LOG2 - SCORE_BOUND_LOG2 - float(np.log2(L)) - 1.0))
    o = lax.cond(bounded,
                 lambda: _attn_call(q3, k3, vt, running_max=False),
                 lambda: _attn_call(q3, k3, vt, running_max=True))
    x2d = _row_call(_attn_out_mlp_kernel, "attn_out_mlp", T, [x2d, o.reshape(T, Q_DIM)],
                    [p["w_o"], p["g_mlp0"], p["w1_0"], p["w2_0"]])
    x2d = _conv_call(x2d, p["g_mix1"], p["w_in"], p["conv_w"], p["w_out"], L)
    y = _row_call(_mlp_final_kernel, "mlp_final_norm", T, [x2d],
                  [p["g_mlp1"], p["w1_1"], p["w2_1"], p["g_final"]])
    return y.reshape(B, L, D_MODEL)


def kernel(x_prompt, x_sample, norm_mix, norm_mlp, w_qkv, q_gain, k_gain, w_o, w_conv_in, conv_w,
           w_conv_out, w_mlp1, w_mlp2, norm_final):
    w_big, w_vt, e_q, e_k, head_gains = _prep_attn_weights(w_qkv[0], q_gain[0], k_gain[0])
    score_bound_log2 = (HEAD_DIM * Q_SCALE * BF16_ROUNDING_MARGIN
                        * jnp.max(jnp.abs(q_gain[0])) * jnp.max(jnp.abs(k_gain[0])))
    w_v_col_norm = jnp.sqrt(jnp.max(jnp.sum(jnp.square(w_qkv[0][:, Q_DIM + KV_DIM:]), axis=0)))
    v_bound_log2 = jnp.log2(D_MODEL ** 0.5 * BF16_ROUNDING_MARGIN
                            * jnp.max(jnp.abs(norm_mix[0])) * w_v_col_norm)
    p = {
        "g_mix0": norm_mix[0:1], "g_mix1": norm_mix[1:2],
        "g_mlp0": norm_mlp[0:1], "g_mlp1": norm_mlp[1:2],
        "g_final": norm_final[None, :],
        "score_bound_log2": score_bound_log2, "v_bound_log2": v_bound_log2,
        "w_big": w_big, "w_vt": w_vt, "e_q": e_q, "e_k": e_k, "head_gains": head_gains,
        "w_o": w_o[0].astype(BF16),
        "w1_0": w_mlp1[0].astype(BF16), "w2_0": w_mlp2[0].astype(BF16),
        "w1_1": w_mlp1[1].astype(BF16), "w2_1": w_mlp2[1].astype(BF16),
        "w_in": w_conv_in[0].astype(BF16), "conv_w": conv_w[0],
        "w_out": w_conv_out[0].astype(BF16),
    }
    return (_trunk(x_prompt, p), _trunk(x_sample, p))
```

```python
import functools

import numpy as np
import jax
import jax.numpy as jnp
from jax import lax
from jax.experimental import pallas as pl
from jax.experimental.pallas import tpu as pltpu

D_MODEL = 1024
N_HEADS = 16
N_KV_HEADS = 4
HEAD_DIM = 64
GQA_GROUP = N_HEADS // N_KV_HEADS
HALF_DIM = HEAD_DIM // 2
KV_DIM = N_KV_HEADS * HEAD_DIM
Q_DIM = N_HEADS * HEAD_DIM
D_FF = 4 * D_MODEL
ROPE_AXIS_DIM = HEAD_DIM // 2
ROPE_THETA = 10000.0
GRID_W = 64
NORM_EPS = 1e-6
CONV_WIDTH = 3

LANE = 128
SUBLANE = 8
MXU_TILE = 256
Q_HALF = Q_DIM // 2
K_HALF = KV_DIM // 2
QK_COLS = Q_DIM + KV_DIM
V_SLAB = N_KV_HEADS * LANE
VMEM_LIMIT_BYTES = 56 * 1024 * 1024
Q_SCALE = HEAD_DIM ** -0.5 * float(np.log2(np.e))
SCORE_BOUND_LOG2 = 60.0
F32_MAX_LOG2 = 126.0
BF16_ROUNDING_MARGIN = (1.0 + 2.0 ** -8) ** 2

ROW_TILE = 512
FF_CHUNK = 1024
ATT_TQ = 256
ATT_TKB = 4096
ATT_CHUNK = {True: (512, 2), False: (2048, 1)}

BF16 = jnp.bfloat16
F32 = jnp.float32


def _const_spec(shape):
    zeros = (0,) * len(shape)
    return pl.BlockSpec(shape, lambda *_: zeros, pipeline_mode=pl.Buffered(1))


def _rms(x, gain):
    return x * lax.rsqrt(jnp.mean(x * x, axis=-1, keepdims=True) + NORM_EPS) * gain


def _qkv_kernel(x_ref, g_ref, w_ref, wvt_ref, e_ref, hg_ref, cos_ref, sin_ref,
                q_ref, k_ref, vt_ref):
    h = _rms(x_ref[...], g_ref[...]).astype(BF16)
    y = jnp.dot(h, w_ref[...], preferred_element_type=F32)
    cos1 = cos_ref[...]
    sin1 = sin_ref[...]

    def norm_rope(a0, a1, g0, g1, scale, out_ref):
        width = a0.shape[1]
        sq = a0 * a0 + a1 * a1
        hi = sq.astype(BF16)
        lo = (sq - hi.astype(F32)).astype(BF16)
        step = min(MXU_TILE, width)
        e = e_ref[:step, :step]
        ss = jnp.concatenate(
            [jnp.dot(hi[:, c:c + step], e, preferred_element_type=F32)
             + jnp.dot(lo[:, c:c + step], e, preferred_element_type=F32)
             for c in range(0, width, step)], axis=1)
        r = lax.rsqrt(ss * (1.0 / HEAD_DIM) + NORM_EPS)
        cos = jnp.concatenate([cos1] * (width // LANE), axis=1)
        sin = jnp.concatenate([sin1] * (width // LANE), axis=1)
        n0 = a0 * r * g0
        n1 = a1 * r * g1
        o0 = ((n0 * cos - n1 * sin) * scale).astype(BF16)
        o1 = ((n0 * sin + n1 * cos) * scale).astype(BF16)
        for t in range(width // LANE):
            out_ref[:, 2 * t * LANE:(2 * t + 1) * LANE] = o0[:, t * LANE:(t + 1) * LANE]
            out_ref[:, (2 * t + 1) * LANE:(2 * t + 2) * LANE] = o1[:, t * LANE:(t + 1) * LANE]

    norm_rope(y[:, 0:Q_HALF], y[:, Q_HALF:Q_DIM], hg_ref[0:1, :], hg_ref[1:2, :], Q_SCALE, q_ref)
    norm_rope(y[:, Q_DIM:Q_DIM + K_HALF], y[:, Q_DIM + K_HALF:QK_COLS],
              hg_ref[2:3, :K_HALF], hg_ref[3:4, :K_HALF], 1.0, k_ref)
    yvt = lax.dot_general(wvt_ref[...], h, (((1,), (1,)), ((), ())), preferred_element_type=F32)
    vrow = lax.broadcasted_iota(jnp.int32, yvt.shape, 0)
    vt_ref[...] = jnp.where(vrow % LANE == HEAD_DIM, 1.0, yvt).astype(BF16)


def _qkv_call(x2d, gain, w_qk, w_vt, e_head, head_gains, cos, sin, seq_len):
    T = x2d.shape[0]
    tm = ROW_TILE
    tiles_per_seq = seq_len // tm
    row = lambda i: (i, 0)
    pos = lambda i: (i % tiles_per_seq, 0)
    return pl.pallas_call(
        _qkv_kernel,
        grid=(T // tm,),
        in_specs=[
            pl.BlockSpec((tm, D_MODEL), row),
            _const_spec((1, D_MODEL)),
            _const_spec((D_MODEL, QK_COLS)),
            _const_spec((V_SLAB, D_MODEL)),
            _const_spec((MXU_TILE, MXU_TILE)),
            _const_spec((4, Q_HALF)),
            pl.BlockSpec((tm, LANE), pos),
            pl.BlockSpec((tm, LANE), pos),
        ],
        out_specs=[
            pl.BlockSpec((tm, Q_DIM), row),
            pl.BlockSpec((tm, KV_DIM), row),
            pl.BlockSpec((V_SLAB, tm), lambda i: (0, i)),
        ],
        out_shape=[
            jax.ShapeDtypeStruct((T, Q_DIM), BF16),
            jax.ShapeDtypeStruct((T, KV_DIM), BF16),
            jax.ShapeDtypeStruct((V_SLAB, T), BF16),
        ],
        compiler_params=pltpu.CompilerParams(
            dimension_semantics=("parallel",), vmem_limit_bytes=VMEM_LIMIT_BYTES),
        name="qkv_norm_rope",
    )(x2d, gain, w_qk, w_vt, e_head, head_gains, cos, sin)


def _attn_kernel(q_ref, k_ref, vt_ref, o_ref, qm_ref, m_ref, acc_ref, *, tq, tk, n_chunks, depth,
                 running_max):
    j = pl.program_id(2)
    qcol = GQA_GROUP * tq

    @pl.when(j == 0)
    def _():
        lane = lax.broadcasted_iota(jnp.int32, (tq, 2 * K_HALF), 1)
        kv_head_of_lane = (lane % K_HALF) // HALF_DIM
        for g in range(GQA_GROUP):
            slab = q_ref[:, g * 2 * K_HALF:(g + 1) * 2 * K_HALF]
            for t in range(N_KV_HEADS):
                h = t * GQA_GROUP + g
                qm_ref[h * tq:(h + 1) * tq, :] = jnp.where(kv_head_of_lane == t, slab,
                                                           jnp.zeros_like(slab))
        m_ref[...] = jnp.full(m_ref.shape, -jnp.inf, F32)
        acc_ref[...] = jnp.zeros(acc_ref.shape, F32)

    units = [(c, t) for c in range(n_chunks) for t in range(N_KV_HEADS)]

    def scores(c, t):
        kc = k_ref[c * tk:(c + 1) * tk, :]
        qt = qm_ref[t * qcol:(t + 1) * qcol, :]
        return lax.dot_general(kc, qt, (((1,), (1,)), ((), ())),
                               preferred_element_type=F32)

    def softmax_pv(s, c, t):
        cols = slice(t * qcol, (t + 1) * qcol)
        vtc = vt_ref[t * LANE:(t + 1) * LANE, c * tk:(c + 1) * tk]
        if not running_max:
            p = jnp.exp2(s).astype(BF16)
            acc_ref[:, cols] += jnp.dot(vtc, p, preferred_element_type=F32)
            return
        m_prev = m_ref[:, cols]
        m_new = jnp.maximum(m_prev, jnp.max(s, axis=0, keepdims=True))
        alpha = jnp.exp2(m_prev - m_new)
        p = jnp.exp2(s - m_new).astype(BF16)
        acc_ref[:, cols] = alpha * acc_ref[:, cols] + jnp.dot(vtc, p, preferred_element_type=F32)
        m_ref[:, cols] = m_new

    pending = [scores(*units[u]) for u in range(min(depth, len(units)))]
    for u, (c, t) in enumerate(units):
        if u + depth < len(units):
            pending.append(scores(*units[u + depth]))
        softmax_pv(pending.pop(0), c, t)

    @pl.when(j == pl.num_programs(2) - 1)
    def _():
        acc = acc_ref[...]
        o = (acc / acc[HEAD_DIM:HEAD_DIM + 1, :]).T
        lane = lax.broadcasted_iota(jnp.int32, (tq, LANE), 1)
        pairs = []
        for h in range(0, N_HEADS, 2):
            lo = o[h * tq:(h + 1) * tq, :]
            hi = pltpu.roll(o[(h + 1) * tq:(h + 2) * tq, :], HEAD_DIM, 1)
            pairs.append(jnp.where(lane < HEAD_DIM, lo, hi))
        o_ref[...] = jnp.concatenate(pairs, axis=1).astype(BF16)


def _attn_call(q, k, vt, running_max):
    B, L, _ = q.shape
    tq = min(ATT_TQ, L)
    tkb = min(ATT_TKB, L)
    tk, depth = ATT_CHUNK[running_max]
    tk = min(tk, tkb)
    kv_blocks = L // tkb
    kern = functools.partial(_attn_kernel, tq=tq, tk=tk, n_chunks=tkb // tk, depth=depth,
                             running_max=running_max)
    return pl.pallas_call(
        kern,
        grid=(B, L // tq, kv_blocks),
        in_specs=[
            pl.BlockSpec((None, tq, Q_DIM), lambda b, i, j: (b, i, 0)),
            pl.BlockSpec((None, tkb, KV_DIM), lambda b, i, j: (b, j, 0)),
            pl.BlockSpec((V_SLAB, tkb), lambda b, i, j: (0, b * kv_blocks + j)),
        ],
        out_specs=pl.BlockSpec((None, tq, Q_DIM), lambda b, i, j: (b, i, 0)),
        out_shape=jax.ShapeDtypeStruct((B, L, Q_DIM), BF16),
        scratch_shapes=[
            pltpu.VMEM((N_HEADS * tq, KV_DIM), BF16),
            pltpu.VMEM((1, N_HEADS * tq), F32),
            pltpu.VMEM((LANE, N_HEADS * tq), F32),
        ],
        compiler_params=pltpu.CompilerParams(
            dimension_semantics=("parallel", "parallel", "arbitrary"),
            vmem_limit_bytes=VMEM_LIMIT_BYTES),
        name="flash_attn" if running_max else "flash_attn_bounded",
    )(q, k, vt)


def _mlp_body(x1, gm, w1_ref, w2_ref):
    h = _rms(x1, gm).astype(BF16)
    acc = jnp.zeros_like(x1)
    for c in range(D_FF // FF_CHUNK):
        cols = slice(c * FF_CHUNK, (c + 1) * FF_CHUNK)
        hid = jnp.maximum(jnp.dot(h, w1_ref[:, cols], preferred_element_type=F32), 0.0)
        acc = acc + jnp.dot((hid * hid).astype(BF16), w2_ref[cols, :], preferred_element_type=F32)
    return x1 + acc


def _attn_out_mlp_kernel(x_ref, o_ref, wo_ref, gm_ref, w1_ref, w2_ref, out_ref):
    x1 = x_ref[...] + jnp.dot(o_ref[...], wo_ref[...], preferred_element_type=F32)
    out_ref[...] = _mlp_body(x1, gm_ref[...], w1_ref, w2_ref)


def _mlp_final_kernel(x_ref, gm_ref, w1_ref, w2_ref, gf_ref, out_ref):
    y = _mlp_body(x_ref[...], gm_ref[...], w1_ref, w2_ref)
    out_ref[...] = _rms(y, gf_ref[...])


def _row_call(kernel, name, T, row_inputs, const_inputs):
    tm = ROW_TILE
    row = lambda i: (i, 0)
    in_specs = [pl.BlockSpec((tm, a.shape[1]), row) for a in row_inputs]
    in_specs += [_const_spec(a.shape) for a in const_inputs]
    return pl.pallas_call(
        kernel,
        grid=(T // tm,),
        in_specs=in_specs,
        out_specs=pl.BlockSpec((tm, D_MODEL), row),
        out_shape=jax.ShapeDtypeStruct((T, D_MODEL), F32),
        compiler_params=pltpu.CompilerParams(
            dimension_semantics=("parallel",), vmem_limit_bytes=VMEM_LIMIT_BYTES),
        name=name,
    )(*row_inputs, *const_inputs)


def _conv_kernel(x_ref, xp_ref, xn_ref, g_ref, win_ref, cw_ref, wout_ref, out_ref, u_ref,
                 *, tm, tiles_per_seq):
    i = pl.program_id(0)
    first = (i % tiles_per_seq) == 0
    last = (i % tiles_per_seq) == tiles_per_seq - 1
    x = x_ref[...]
    xa = jnp.concatenate([xp_ref[...], x, xn_ref[...]], axis=0)
    h = _rms(xa, g_ref[...]).astype(BF16)
    bcx = jnp.dot(h, win_ref[...], preferred_element_type=F32)
    u = bcx[:, D_MODEL:2 * D_MODEL] * bcx[:, 2 * D_MODEL:]
    r = lax.broadcasted_iota(jnp.int32, (tm + 2 * SUBLANE, 1), 0)
    pad = (first & (r < SUBLANE)) | (last & (r >= tm + SUBLANE))
    u_ref[...] = jnp.where(pad, 0.0, u)
    z = (u_ref[pl.ds(SUBLANE - 1, tm), :] * cw_ref[0:1, :]
         + u_ref[pl.ds(SUBLANE, tm), :] * cw_ref[1:2, :]
         + u_ref[pl.ds(SUBLANE + 1, tm), :] * cw_ref[2:3, :])
    gated = (bcx[SUBLANE:SUBLANE + tm, :D_MODEL] * z).astype(BF16)
    out_ref[...] = x + jnp.dot(gated, wout_ref[...], preferred_element_type=F32)


def _conv_call(x2d, gain, w_in, conv_w, w_out, seq_len):
    T = x2d.shape[0]
    tm = ROW_TILE
    halo_blocks = tm // SUBLANE
    n_halo = T // SUBLANE
    kern = functools.partial(_conv_kernel, tm=tm, tiles_per_seq=seq_len // tm)
    return pl.pallas_call(
        kern,
        grid=(T // tm,),
        in_specs=[
            pl.BlockSpec((tm, D_MODEL), lambda i: (i, 0)),
            pl.BlockSpec((SUBLANE, D_MODEL), lambda i: (jnp.maximum(i * halo_blocks - 1, 0), 0)),
            pl.BlockSpec((SUBLANE, D_MODEL),
                         lambda i: (jnp.minimum((i + 1) * halo_blocks, n_halo - 1), 0)),
            _const_spec((1, D_MODEL)),
            _const_spec((D_MODEL, 3 * D_MODEL)),
            _const_spec((CONV_WIDTH, D_MODEL)),
            _const_spec((D_MODEL, D_MODEL)),
        ],
        out_specs=pl.BlockSpec((tm, D_MODEL), lambda i: (i, 0)),
        out_shape=jax.ShapeDtypeStruct((T, D_MODEL), F32),
        scratch_shapes=[pltpu.VMEM((tm + 2 * SUBLANE, D_MODEL), F32)],
        compiler_params=pltpu.CompilerParams(
            dimension_semantics=("parallel",), vmem_limit_bytes=VMEM_LIMIT_BYTES),
        name="gated_conv",
    )(x2d, x2d, x2d, gain, w_in, conv_w, w_out)


def _prep_attn_weights(w_qkv, q_gain, k_gain):
    g = np.arange(GQA_GROUP)[:, None, None]
    t = np.arange(N_KV_HEADS)[None, :, None]
    i = np.arange(HALF_DIM)[None, None, :]
    q0 = ((t * GQA_GROUP + g) * HEAD_DIM + 2 * i).reshape(-1)
    k0 = (Q_DIM + t * HEAD_DIM + 2 * i).reshape(-1)
    cols = np.concatenate([q0, q0 + 1, k0, k0 + 1])
    w_qk = jnp.take(w_qkv, jnp.asarray(cols), axis=1).astype(BF16)
    w_v = w_qkv[:, Q_DIM + KV_DIM:].reshape(D_MODEL, N_KV_HEADS, HEAD_DIM)
    w_v = jnp.pad(w_v, ((0, 0), (0, 0), (0, LANE - HEAD_DIM))).reshape(D_MODEL, V_SLAB)
    w_vt = w_v.T.astype(BF16)

    lane = np.arange(MXU_TILE)
    e_head = (lane[:, None] // HALF_DIM == lane[None, :] // HALF_DIM).astype(np.float32)
    reps = Q_HALF // HALF_DIM
    head_gains = jnp.stack([jnp.tile(q_gain[0::2], reps), jnp.tile(q_gain[1::2], reps),
                            jnp.tile(k_gain[0::2], reps), jnp.tile(k_gain[1::2], reps)])
    return w_qk, w_vt, jnp.asarray(e_head, BF16), head_gains


def _rope_tables(length):
    rows = length // GRID_W
    row_ids = jnp.repeat(jnp.arange(rows, dtype=F32), GRID_W)
    col_ids = jnp.tile(jnp.arange(GRID_W, dtype=F32), rows)
    inv_freq = ROPE_THETA ** (-jnp.arange(0, ROPE_AXIS_DIM, 2, dtype=F32) / ROPE_AXIS_DIM)
    ang = jnp.concatenate([row_ids[:, None] * inv_freq[None, :],
                           col_ids[:, None] * inv_freq[None, :]], axis=-1)
    reps = LANE // HALF_DIM
    return jnp.tile(jnp.cos(ang), (1, reps)), jnp.tile(jnp.sin(ang), (1, reps))


def _trunk(x, p):
    B, L, _ = x.shape
    T = B * L
    x2d = x.reshape(T, D_MODEL)
    cos, sin = _rope_tables(L)
    q, k, vt = _qkv_call(x2d, p["g_mix0"], p["w_qk"], p["w_vt"], p["e_head"], p["head_gains"],
                         cos, sin, L)
    q3, k3 = q.reshape(B, L, -1), k.reshape(B, L, -1)
    bounded = ((p["score_bound_log2"] <= SCORE_BOUND_LOG2)
               & (p["v_bound_log2"] <= F32_MAX_LOG2 - SCORE_BOUND_LOG2 - float(np.log2(L)) - 1.0))
    o = lax.cond(bounded,
                 lambda: _attn_call(q3, k3, vt, running_max=False),
                 lambda: _attn_call(q3, k3, vt, running_max=True))
    x2d = _row_call(_attn_out_mlp_kernel, "attn_out_mlp", T, [x2d, o.reshape(T, Q_DIM)],
                    [p["w_o"], p["g_mlp0"], p["w1_0"], p["w2_0"]])
    x2d = _conv_call(x2d, p["g_mix1"], p["w_in"], p["conv_w"], p["w_out"], L)
    y = _row_call(_mlp_final_kernel, "mlp_final_norm", T, [x2d],
                  [p["g_mlp1"], p["w1_1"], p["w2_1"], p["g_final"]])
    return y.reshape(B, L, D_MODEL)


def kernel(x_prompt, x_sample, norm_mix, norm_mlp, w_qkv, q_gain, k_gain, w_o, w_conv_in, conv_w,
           w_conv_out, w_mlp1, w_mlp2, norm_final):
    w_qk, w_vt, e_head, head_gains = _prep_attn_weights(w_qkv[0], q_gain[0], k_gain[0])
    score_bound_log2 = (HEAD_DIM * Q_SCALE * BF16_ROUNDING_MARGIN
                        * jnp.max(jnp.abs(q_gain[0])) * jnp.max(jnp.abs(k_gain[0])))
    w_v_col_norm = jnp.sqrt(jnp.max(jnp.sum(jnp.square(w_qkv[0][:, Q_DIM + KV_DIM:]), axis=0)))
    v_bound_log2 = jnp.log2(D_MODEL ** 0.5 * BF16_ROUNDING_MARGIN
                            * jnp.max(jnp.abs(norm_mix[0])) * w_v_col_norm)
    p = {
        "g_mix0": norm_mix[0:1], "g_mix1": norm_mix[1:2],
        "g_mlp0": norm_mlp[0:1], "g_mlp1": norm_mlp[1:2],
        "g_final": norm_final[None, :],
        "score_bound_log2": score_bound_log2, "v_bound_log2": v_bound_log2,
        "w_qk": w_qk, "w_vt": w_vt, "e_head": e_head, "head_gains": head_gains,
        "w_o": w_o[0].astype(BF16),
        "w1_0": w_mlp1[0].astype(BF16), "w2_0": w_mlp2[0].astype(BF16),
        "w1_1": w_mlp1[1].astype(BF16), "w2_1": w_mlp2[1].astype(BF16),
        "w_in": w_conv_in[0].astype(BF16), "conv_w": conv_w[0],
        "w_out": w_conv_out[0].astype(BF16),
    }
    return (_trunk(x_prompt, p), _trunk(x_sample, p))
```

```python
import functools

import numpy as np
import jax
import jax.numpy as jnp
from jax import lax
from jax.experimental import pallas as pl
from jax.experimental.pallas import tpu as pltpu

D_MODEL = 1024
N_HEADS = 16
N_KV_HEADS = 4
HEAD_DIM = 64
GQA_GROUP = N_HEADS // N_KV_HEADS
HALF_DIM = HEAD_DIM // 2
KV_DIM = N_KV_HEADS * HEAD_DIM
Q_DIM = N_HEADS * HEAD_DIM
D_FF = 4 * D_MODEL
ROPE_AXIS_DIM = HEAD_DIM // 2
ROPE_THETA = 10000.0
GRID_W = 64
NORM_EPS = 1e-6
CONV_WIDTH = 3

LANE = 128
SUBLANE = 8
MXU_TILE = 256
Q_HALF = Q_DIM // 2
K_HALF = KV_DIM // 2
QK_COLS = Q_DIM + KV_DIM
V_SLAB = N_KV_HEADS * LANE
VMEM_LIMIT_BYTES = 56 * 1024 * 1024
Q_SCALE = HEAD_DIM ** -0.5 * float(np.log2(np.e))
SCORE_BOUND_LOG2 = 1.0
F32_MAX_LOG2 = 126.0
BF16_ROUNDING_MARGIN = (1.0 + 2.0 ** -8) ** 2

ROW_TILE = 512
FF_CHUNK = 1024
ATT_TQ = 256
ATT_TKB = 4096
ATT_CHUNK = {True: (512, 2), False: (2048, 1)}

BF16 = jnp.bfloat16
F32 = jnp.float32


def _const_spec(shape):
    zeros = (0,) * len(shape)
    return pl.BlockSpec(shape, lambda *_: zeros, pipeline_mode=pl.Buffered(1))


def _rms(x, gain):
    return x * lax.rsqrt(jnp.mean(x * x, axis=-1, keepdims=True) + NORM_EPS) * gain


def _qkv_kernel(x_ref, g_ref, w_ref, wvt_ref, e_ref, hg_ref, cos_ref, sin_ref,
                q_ref, k_ref, vt_ref):
    h = _rms(x_ref[...], g_ref[...]).astype(BF16)
    y = jnp.dot(h, w_ref[...], preferred_element_type=F32)
    cos1 = cos_ref[...]
    sin1 = sin_ref[...]

    def norm_rope(a0, a1, g0, g1, scale, out_ref):
        width = a0.shape[1]
        sq = a0 * a0 + a1 * a1
        hi = sq.astype(BF16)
        lo = (sq - hi.astype(F32)).astype(BF16)
        step = min(MXU_TILE, width)
        e = e_ref[:step, :step]
        ss = jnp.concatenate(
            [jnp.dot(hi[:, c:c + step], e, preferred_element_type=F32)
             + jnp.dot(lo[:, c:c + step], e, preferred_element_type=F32)
             for c in range(0, width, step)], axis=1)
        r = lax.rsqrt(ss * (1.0 / HEAD_DIM) + NORM_EPS)
        cos = jnp.concatenate([cos1] * (width // LANE), axis=1)
        sin = jnp.concatenate([sin1] * (width // LANE), axis=1)
        n0 = a0 * r * g0
        n1 = a1 * r * g1
        o0 = ((n0 * cos - n1 * sin) * scale).astype(BF16)
        o1 = ((n0 * sin + n1 * cos) * scale).astype(BF16)
        for t in range(width // LANE):
            out_ref[:, 2 * t * LANE:(2 * t + 1) * LANE] = o0[:, t * LANE:(t + 1) * LANE]
            out_ref[:, (2 * t + 1) * LANE:(2 * t + 2) * LANE] = o1[:, t * LANE:(t + 1) * LANE]

    norm_rope(y[:, 0:Q_HALF], y[:, Q_HALF:Q_DIM], hg_ref[0:1, :], hg_ref[1:2, :], Q_SCALE, q_ref)
    norm_rope(y[:, Q_DIM:Q_DIM + K_HALF], y[:, Q_DIM + K_HALF:QK_COLS],
              hg_ref[2:3, :K_HALF], hg_ref[3:4, :K_HALF], 1.0, k_ref)
    yvt = lax.dot_general(wvt_ref[...], h, (((1,), (1,)), ((), ())), preferred_element_type=F32)
    vrow = lax.broadcasted_iota(jnp.int32, yvt.shape, 0)
    vt_ref[...] = jnp.where(vrow % LANE == HEAD_DIM, 1.0, yvt).astype(BF16)


def _qkv_call(x2d, gain, w_qk, w_vt, e_head, head_gains, cos, sin, seq_len):
    T = x2d.shape[0]
    tm = ROW_TILE
    tiles_per_seq = seq_len // tm
    row = lambda i: (i, 0)
    pos = lambda i: (i % tiles_per_seq, 0)
    return pl.pallas_call(
        _qkv_kernel,
        grid=(T // tm,),
        in_specs=[
            pl.BlockSpec((tm, D_MODEL), row),
            _const_spec((1, D_MODEL)),
            _const_spec((D_MODEL, QK_COLS)),
            _const_spec((V_SLAB, D_MODEL)),
            _const_spec((MXU_TILE, MXU_TILE)),
            _const_spec((4, Q_HALF)),
            pl.BlockSpec((tm, LANE), pos),
            pl.BlockSpec((tm, LANE), pos),
        ],
        out_specs=[
            pl.BlockSpec((tm, Q_DIM), row),
            pl.BlockSpec((tm, KV_DIM), row),
            pl.BlockSpec((V_SLAB, tm), lambda i: (0, i)),
        ],
        out_shape=[
            jax.ShapeDtypeStruct((T, Q_DIM), BF16),
            jax.ShapeDtypeStruct((T, KV_DIM), BF16),
            jax.ShapeDtypeStruct((V_SLAB, T), BF16),
        ],
        compiler_params=pltpu.CompilerParams(
            dimension_semantics=("parallel",), vmem_limit_bytes=VMEM_LIMIT_BYTES),
        name="qkv_norm_rope",
    )(x2d, gain, w_qk, w_vt, e_head, head_gains, cos, sin)


def _attn_kernel(q_ref, k_ref, vt_ref, o_ref, qm_ref, m_ref, acc_ref, *, tq, tk, n_chunks, depth,
                 running_max):
    j = pl.program_id(2)
    qcol = GQA_GROUP * tq

    @pl.when(j == 0)
    def _():
        lane = lax.broadcasted_iota(jnp.int32, (tq, 2 * K_HALF), 1)
        kv_head_of_lane = (lane % K_HALF) // HALF_DIM
        for g in range(GQA_GROUP):
            slab = q_ref[:, g * 2 * K_HALF:(g + 1) * 2 * K_HALF]
            for t in range(N_KV_HEADS):
                h = t * GQA_GROUP + g
                qm_ref[h * tq:(h + 1) * tq, :] = jnp.where(kv_head_of_lane == t, slab,
                                                           jnp.zeros_like(slab))
        m_ref[...] = jnp.full(m_ref.shape, -jnp.inf, F32)
        acc_ref[...] = jnp.zeros(acc_ref.shape, F32)

    units = [(c, t) for c in range(n_chunks) for t in range(N_KV_HEADS)]

    def scores(c, t):
        kc = k_ref[c * tk:(c + 1) * tk, :]
        qt = qm_ref[t * qcol:(t + 1) * qcol, :]
        return lax.dot_general(kc, qt, (((1,), (1,)), ((), ())),
                               preferred_element_type=F32)

    def softmax_pv(s, c, t):
        cols = slice(t * qcol, (t + 1) * qcol)
        vtc = vt_ref[t * LANE:(t + 1) * LANE, c * tk:(c + 1) * tk]
        if not running_max:
            p = jnp.exp2(s).astype(BF16)
            acc_ref[:, cols] += jnp.dot(vtc, p, preferred_element_type=F32)
            return
        m_prev = m_ref[:, cols]
        m_new = jnp.maximum(m_prev, jnp.max(s, axis=0, keepdims=True))
        alpha = jnp.exp2(m_prev - m_new)
        p = jnp.exp2(s - m_new).astype(BF16)
        acc_ref[:, cols] = alpha * acc_ref[:, cols] + jnp.dot(vtc, p, preferred_element_type=F32)
        m_ref[:, cols] = m_new

    pending = [scores(*units[u]) for u in range(min(depth, len(units)))]
    for u, (c, t) in enumerate(units):
        if u + depth < len(units):
            pending.append(scores(*units[u + depth]))
        softmax_pv(pending.pop(0), c, t)

    @pl.when(j == pl.num_programs(2) - 1)
    def _():
        acc = acc_ref[...]
        o = (acc / acc[HEAD_DIM:HEAD_DIM + 1, :]).T
        lane = lax.broadcasted_iota(jnp.int32, (tq, LANE), 1)
        pairs = []
        for h in range(0, N_HEADS, 2):
            lo = o[h * tq:(h + 1) * tq, :]
            hi = pltpu.roll(o[(h + 1) * tq:(h + 2) * tq, :], HEAD_DIM, 1)
            pairs.append(jnp.where(lane < HEAD_DIM, lo, hi))
        o_ref[...] = jnp.concatenate(pairs, axis=1).astype(BF16)


def _attn_call(q, k, vt, running_max):
    B, L, _ = q.shape
    tq = min(ATT_TQ, L)
    tkb = min(ATT_TKB, L)
    tk, depth = ATT_CHUNK[running_max]
    tk = min(tk, tkb)
    kv_blocks = L // tkb
    kern = functools.partial(_attn_kernel, tq=tq, tk=tk, n_chunks=tkb // tk, depth=depth,
                             running_max=running_max)
    return pl.pallas_call(
        kern,
        grid=(B, L // tq, kv_blocks),
        in_specs=[
            pl.BlockSpec((None, tq, Q_DIM), lambda b, i, j: (b, i, 0)),
            pl.BlockSpec((None, tkb, KV_DIM), lambda b, i, j: (b, j, 0)),
            pl.BlockSpec((V_SLAB, tkb), lambda b, i, j: (0, b * kv_blocks + j)),
        ],
        out_specs=pl.BlockSpec((None, tq, Q_DIM), lambda b, i, j: (b, i, 0)),
        out_shape=jax.ShapeDtypeStruct((B, L, Q_DIM), BF16),
        scratch_shapes=[
            pltpu.VMEM((N_HEADS * tq, KV_DIM), BF16),
            pltpu.VMEM((1, N_HEADS * tq), F32),
            pltpu.VMEM((LANE, N_HEADS * tq), F32),
        ],
        compiler_params=pltpu.CompilerParams(
            dimension_semantics=("parallel", "parallel", "arbitrary"),
            vmem_limit_bytes=VMEM_LIMIT_BYTES),
        name="flash_attn" if running_max else "flash_attn_bounded",
    )(q, k, vt)


def _mlp_body(x1, gm, w1_ref, w2_ref):
    h = _rms(x1, gm).astype(BF16)
    acc = jnp.zeros_like(x1)
    for c in range(D_FF // FF_CHUNK):
        cols = slice(c * FF_CHUNK, (c + 1) * FF_CHUNK)
        hid = jnp.maximum(jnp.dot(h, w1_ref[:, cols], preferred_element_type=F32), 0.0)
        acc = acc + jnp.dot((hid * hid).astype(BF16), w2_ref[cols, :], preferred_element_type=F32)
    return x1 + acc


def _attn_out_mlp_kernel(x_ref, o_ref, wo_ref, gm_ref, w1_ref, w2_ref, out_ref):
    x1 = x_ref[...] + jnp.dot(o_ref[...], wo_ref[...], preferred_element_type=F32)
    out_ref[...] = _mlp_body(x1, gm_ref[...], w1_ref, w2_ref)


def _mlp_final_kernel(x_ref, gm_ref, w1_ref, w2_ref, gf_ref, out_ref):
    y = _mlp_body(x_ref[...], gm_ref[...], w1_ref, w2_ref)
    out_ref[...] = _rms(y, gf_ref[...])


def _row_call(kernel, name, T, row_inputs, const_inputs):
    tm = ROW_TILE
    row = lambda i: (i, 0)
    in_specs = [pl.BlockSpec((tm, a.shape[1]), row) for a in row_inputs]
    in_specs += [_const_spec(a.shape) for a in const_inputs]
    return pl.pallas_call(
        kernel,
        grid=(T // tm,),
        in_specs=in_specs,
        out_specs=pl.BlockSpec((tm, D_MODEL), row),
        out_shape=jax.ShapeDtypeStruct((T, D_MODEL), F32),
        compiler_params=pltpu.CompilerParams(
            dimension_semantics=("parallel",), vmem_limit_bytes=VMEM_LIMIT_BYTES),
        name=name,
    )(*row_inputs, *const_inputs)


def _conv_kernel(x_ref, xp_ref, xn_ref, g_ref, win_ref, cw_ref, wout_ref, out_ref, u_ref,
                 *, tm, tiles_per_seq):
    i = pl.program_id(0)
    first = (i % tiles_per_seq) == 0
    last = (i % tiles_per_seq) == tiles_per_seq - 1
    x = x_ref[...]
    xa = jnp.concatenate([xp_ref[...], x, xn_ref[...]], axis=0)
    h = _rms(xa, g_ref[...]).astype(BF16)
    bcx = jnp.dot(h, win_ref[...], preferred_element_type=F32)
    u = bcx[:, D_MODEL:2 * D_MODEL] * bcx[:, 2 * D_MODEL:]
    r = lax.broadcasted_iota(jnp.int32, (tm + 2 * SUBLANE, 1), 0)
    pad = (first & (r < SUBLANE)) | (last & (r >= tm + SUBLANE))
    u_ref[...] = jnp.where(pad, 0.0, u)
    z = (u_ref[pl.ds(SUBLANE - 1, tm), :] * cw_ref[0:1, :]
         + u_ref[pl.ds(SUBLANE, tm), :] * cw_ref[1:2, :]
         + u_ref[pl.ds(SUBLANE + 1, tm), :] * cw_ref[2:3, :])
    gated = (bcx[SUBLANE:SUBLANE + tm, :D_MODEL] * z).astype(BF16)
    out_ref[...] = x + jnp.dot(gated, wout_ref[...], preferred_element_type=F32)


def _conv_call(x2d, gain, w_in, conv_w, w_out, seq_len):
    T = x2d.shape[0]
    tm = ROW_TILE
    halo_blocks = tm // SUBLANE
    n_halo = T // SUBLANE
    kern = functools.partial(_conv_kernel, tm=tm, tiles_per_seq=seq_len // tm)
    return pl.pallas_call(
        kern,
        grid=(T // tm,),
        in_specs=[
            pl.BlockSpec((tm, D_MODEL), lambda i: (i, 0)),
            pl.BlockSpec((SUBLANE, D_MODEL), lambda i: (jnp.maximum(i * halo_blocks - 1, 0), 0)),
            pl.BlockSpec((SUBLANE, D_MODEL),
                         lambda i: (jnp.minimum((i + 1) * halo_blocks, n_halo - 1), 0)),
            _const_spec((1, D_MODEL)),
            _const_spec((D_MODEL, 3 * D_MODEL)),
            _const_spec((CONV_WIDTH, D_MODEL)),
            _const_spec((D_MODEL, D_MODEL)),
        ],
        out_specs=pl.BlockSpec((tm, D_MODEL), lambda i: (i, 0)),
        out_shape=jax.ShapeDtypeStruct((T, D_MODEL), F32),
        scratch_shapes=[pltpu.VMEM((tm + 2 * SUBLANE, D_MODEL), F32)],
        compiler_params=pltpu.CompilerParams(
            dimension_semantics=("parallel",), vmem_limit_bytes=VMEM_LIMIT_BYTES),
        name="gated_conv",
    )(x2d, x2d, x2d, gain, w_in, conv_w, w_out)


def _prep_attn_weights(w_qkv, q_gain, k_gain):
    g = np.arange(GQA_GROUP)[:, None, None]
    t = np.arange(N_KV_HEADS)[None, :, None]
    i = np.arange(HALF_DIM)[None, None, :]
    q0 = ((t * GQA_GROUP + g) * HEAD_DIM + 2 * i).reshape(-1)
    k0 = (Q_DIM + t * HEAD_DIM + 2 * i).reshape(-1)
    cols = np.concatenate([q0, q0 + 1, k0, k0 + 1])
    w_qk = jnp.take(w_qkv, jnp.asarray(cols), axis=1).astype(BF16)
    w_v = w_qkv[:, Q_DIM + KV_DIM:].reshape(D_MODEL, N_KV_HEADS, HEAD_DIM)
    w_v = jnp.pad(w_v, ((0, 0), (0, 0), (0, LANE - HEAD_DIM))).reshape(D_MODEL, V_SLAB)
    w_vt = w_v.T.astype(BF16)

    lane = np.arange(MXU_TILE)
    e_head = (lane[:, None] // HALF_DIM == lane[None, :] // HALF_DIM).astype(np.float32)
    reps = Q_HALF // HALF_DIM
    head_gains = jnp.stack([jnp.tile(q_gain[0::2], reps), jnp.tile(q_gain[1::2], reps),
                            jnp.tile(k_gain[0::2], reps), jnp.tile(k_gain[1::2], reps)])
    return w_qk, w_vt, jnp.asarray(e_head, BF16), head_gains


def _rope_tables(length):
    rows = length // GRID_W
    row_ids = jnp.repeat(jnp.arange(rows, dtype=F32), GRID_W)
    col_ids = jnp.tile(jnp.arange(GRID_W, dtype=F32), rows)
    inv_freq = ROPE_THETA ** (-jnp.arange(0, ROPE_AXIS_DIM, 2, dtype=F32) / ROPE_AXIS_DIM)
    ang = jnp.concatenate([row_ids[:, None] * inv_freq[None, :],
                           col_ids[:, None] * inv_freq[None, :]], axis=-1)
    reps = LANE // HALF_DIM
    return jnp.tile(jnp.cos(ang), (1, reps)), jnp.tile(jnp.sin(ang), (1, reps))


def _trunk(x, p):
    B, L, _ = x.shape
    T = B * L
    x2d = x.reshape(T, D_MODEL)
    cos, sin = _rope_tables(L)
    q, k, vt = _qkv_call(x2d, p["g_mix0"], p["w_qk"], p["w_vt"], p["e_head"], p["head_gains"],
                         cos, sin, L)
    q3, k3 = q.reshape(B, L, -1), k.reshape(B, L, -1)
    bounded = ((p["score_bound_log2"] <= SCORE_BOUND_LOG2)
               & (p["v_bound_log2"] <= F32_MAX_LOG2 - SCORE_BOUND_LOG2 - float(np.log2(L)) - 1.0))
    o = lax.cond(bounded,
                 lambda: _attn_call(q3, k3, vt, running_max=False),
                 lambda: _attn_call(q3, k3, vt, running_max=True))
    x2d = _row_call(_attn_out_mlp_kernel, "attn_out_mlp", T, [x2d, o.reshape(T, Q_DIM)],
                    [p["w_o"], p["g_mlp0"], p["w1_0"], p["w2_0"]])
    x2d = _conv_call(x2d, p["g_mix1"], p["w_in"], p["conv_w"], p["w_out"], L)
    y = _row_call(_mlp_final_kernel, "mlp_final_norm", T, [x2d],
                  [p["g_mlp1"], p["w1_1"], p["w2_1"], p["g_final"]])
    return y.reshape(B, L, D_MODEL)


def kernel(x_prompt, x_sample, norm_mix, norm_mlp, w_qkv, q_gain, k_gain, w_o, w_conv_in, conv_w,
           w_conv_out, w_mlp1, w_mlp2, norm_final):
    w_qk, w_vt, e_head, head_gains = _prep_attn_weights(w_qkv[0], q_gain[0], k_gain[0])
    score_bound_log2 = (HEAD_DIM * Q_SCALE * BF16_ROUNDING_MARGIN
                        * jnp.max(jnp.abs(q_gain[0])) * jnp.max(jnp.abs(k_gain[0])))
    w_v_col_norm = jnp.sqrt(jnp.max(jnp.sum(jnp.square(w_qkv[0][:, Q_DIM + KV_DIM:]), axis=0)))
    v_bound_log2 = jnp.log2(D_MODEL ** 0.5 * BF16_ROUNDING_MARGIN
                            * jnp.max(jnp.abs(norm_mix[0])) * w_v_col_norm)
    p = {
        "g_mix0": norm_mix[0:1], "g_mix1": norm_mix[1:2],
        "g_mlp0": norm_mlp[0:1], "g_mlp1": norm_mlp[1:2],
        "g_final": norm_final[None, :],
        "score_bound_log2": score_bound_log2, "v_bound_log2": v_bound_log2,
        "w_qk": w_qk, "w_vt": w_vt, "e_head": e_head, "head_gains": head_gains,
        "w_o": w_o[0].astype(BF16),
        "w1_0": w_mlp1[0].astype(BF16), "w2_0": w_mlp2[0].astype(BF16),
        "w1_1": w_mlp1[1].astype(BF16), "w2_1": w_mlp2[1].astype(BF16),
        "w_in": w_conv_in[0].astype(BF16), "conv_w": conv_w[0],
        "w_out": w_conv_out[0].astype(BF16),
    }
    return (_trunk(x_prompt, p), _trunk(x_sample, p))
```

```python
import functools

import numpy as np
import jax
import jax.numpy as jnp
from jax import lax
from jax.experimental import pallas as pl
from jax.experimental.pallas import tpu as pltpu

D_MODEL = 1024
N_HEADS = 16
N_KV_HEADS = 4
HEAD_DIM = 64
GQA_GROUP = N_HEADS // N_KV_HEADS
HALF_DIM = HEAD_DIM // 2
KV_DIM = N_KV_HEADS * HEAD_DIM
Q_DIM = N_HEADS * HEAD_DIM
D_FF = 4 * D_MODEL
ROPE_AXIS_DIM = HEAD_DIM // 2
ROPE_THETA = 10000.0
GRID_W = 64
NORM_EPS = 1e-6
CONV_WIDTH = 3

LANE = 128
SUBLANE = 8
MXU_TILE = 256
Q_HALF = Q_DIM // 2
K_HALF = KV_DIM // 2
QK_COLS = Q_DIM + KV_DIM
V_SLAB = N_KV_HEADS * LANE
VMEM_LIMIT_BYTES = 56 * 1024 * 1024
Q_SCALE = HEAD_DIM ** -0.5 * float(np.log2(np.e))
SCORE_BOUND_LOG2 = 60.0
F32_MAX_LOG2 = 126.0
BF16_ROUNDING_MARGIN = (1.0 + 2.0 ** -8) ** 2

ROW_TILE = 1024
ROW_SPLIT = 2
FF_CHUNK = 1024
ATT_TQ = 256
ATT_TKB = 4096
ATT_CHUNK = {True: (512, 2), False: (2048, 1)}

BF16 = jnp.bfloat16
F32 = jnp.float32


def _const_spec(shape):
    zeros = (0,) * len(shape)
    return pl.BlockSpec(shape, lambda *_: zeros, pipeline_mode=pl.Buffered(1))


def _rms(x, gain):
    return x * lax.rsqrt(jnp.mean(x * x, axis=-1, keepdims=True) + NORM_EPS) * gain


def _qkv_kernel(x_ref, g_ref, w_ref, wvt_ref, e_ref, hg_ref, cos_ref, sin_ref,
                q_ref, k_ref, vt_ref):
    for rows in _row_groups(x_ref.shape[0]):
        _qkv_rows(rows, x_ref, g_ref, w_ref, wvt_ref, e_ref, hg_ref, cos_ref, sin_ref,
                  q_ref, k_ref, vt_ref)


def _row_groups(tm):
    rm = tm // ROW_SPLIT
    return [slice(r * rm, (r + 1) * rm) for r in range(ROW_SPLIT)]


def _qkv_rows(rows, x_ref, g_ref, w_ref, wvt_ref, e_ref, hg_ref, cos_ref, sin_ref,
              q_ref, k_ref, vt_ref):
    h = _rms(x_ref[rows, :], g_ref[...]).astype(BF16)
    y = jnp.dot(h, w_ref[...], preferred_element_type=F32)
    cos1 = cos_ref[rows, :]
    sin1 = sin_ref[rows, :]

    def norm_rope(a0, a1, g0, g1, scale, out_ref):
        width = a0.shape[1]
        sq = a0 * a0 + a1 * a1
        hi = sq.astype(BF16)
        lo = (sq - hi.astype(F32)).astype(BF16)
        step = min(MXU_TILE, width)
        e = e_ref[:step, :step]
        ss = jnp.concatenate(
            [jnp.dot(hi[:, c:c + step], e, preferred_element_type=F32)
             + jnp.dot(lo[:, c:c + step], e, preferred_element_type=F32)
             for c in range(0, width, step)], axis=1)
        r = lax.rsqrt(ss * (1.0 / HEAD_DIM) + NORM_EPS)
        cos = jnp.concatenate([cos1] * (width // LANE), axis=1)
        sin = jnp.concatenate([sin1] * (width // LANE), axis=1)
        n0 = a0 * r * g0
        n1 = a1 * r * g1
        o0 = ((n0 * cos - n1 * sin) * scale).astype(BF16)
        o1 = ((n0 * sin + n1 * cos) * scale).astype(BF16)
        for t in range(width // LANE):
            out_ref[rows, 2 * t * LANE:(2 * t + 1) * LANE] = o0[:, t * LANE:(t + 1) * LANE]
            out_ref[rows, (2 * t + 1) * LANE:(2 * t + 2) * LANE] = o1[:, t * LANE:(t + 1) * LANE]

    norm_rope(y[:, 0:Q_HALF], y[:, Q_HALF:Q_DIM], hg_ref[0:1, :], hg_ref[1:2, :], Q_SCALE, q_ref)
    norm_rope(y[:, Q_DIM:Q_DIM + K_HALF], y[:, Q_DIM + K_HALF:QK_COLS],
              hg_ref[2:3, :K_HALF], hg_ref[3:4, :K_HALF], 1.0, k_ref)
    yvt = lax.dot_general(wvt_ref[...], h, (((1,), (1,)), ((), ())), preferred_element_type=F32)
    vrow = lax.broadcasted_iota(jnp.int32, yvt.shape, 0)
    vt_ref[:, rows] = jnp.where(vrow % LANE == HEAD_DIM, 1.0, yvt).astype(BF16)


def _qkv_call(x2d, gain, w_qk, w_vt, e_head, head_gains, cos, sin, seq_len):
    T = x2d.shape[0]
    tm = ROW_TILE
    tiles_per_seq = seq_len // tm
    row = lambda i: (i, 0)
    pos = lambda i: (i % tiles_per_seq, 0)
    return pl.pallas_call(
        _qkv_kernel,
        grid=(T // tm,),
        in_specs=[
            pl.BlockSpec((tm, D_MODEL), row),
            _const_spec((1, D_MODEL)),
            _const_spec((D_MODEL, QK_COLS)),
            _const_spec((V_SLAB, D_MODEL)),
            _const_spec((MXU_TILE, MXU_TILE)),
            _const_spec((4, Q_HALF)),
            pl.BlockSpec((tm, LANE), pos),
            pl.BlockSpec((tm, LANE), pos),
        ],
        out_specs=[
            pl.BlockSpec((tm, Q_DIM), row),
            pl.BlockSpec((tm, KV_DIM), row),
            pl.BlockSpec((V_SLAB, tm), lambda i: (0, i)),
        ],
        out_shape=[
            jax.ShapeDtypeStruct((T, Q_DIM), BF16),
            jax.ShapeDtypeStruct((T, KV_DIM), BF16),
            jax.ShapeDtypeStruct((V_SLAB, T), BF16),
        ],
        compiler_params=pltpu.CompilerParams(
            dimension_semantics=("parallel",), vmem_limit_bytes=VMEM_LIMIT_BYTES),
        name="qkv_norm_rope",
    )(x2d, gain, w_qk, w_vt, e_head, head_gains, cos, sin)


def _attn_kernel(q_ref, k_ref, vt_ref, o_ref, m_ref, acc_ref, *, tq, tk, n_chunks, depth,
                 running_max):
    j = pl.program_id(2)
    qcol = GQA_GROUP * tq

    @pl.when(j == 0)
    def _():
        m_ref[...] = jnp.full(m_ref.shape, -jnp.inf, F32)
        acc_ref[...] = jnp.zeros(acc_ref.shape, F32)

    units = [(c, t) for c in range(n_chunks) for t in range(N_KV_HEADS)]
    lane = lax.broadcasted_iota(jnp.int32, (tq, 2 * K_HALF), 1)
    kv_head_of_lane = (lane % K_HALF) // HALF_DIM

    def scores(c, t):
        kc = k_ref[c * tk:(c + 1) * tk, :]
        slabs = [q_ref[:, g * 2 * K_HALF:(g + 1) * 2 * K_HALF] for g in range(GQA_GROUP)]
        qt = jnp.concatenate([jnp.where(kv_head_of_lane == t, s, jnp.zeros_like(s)) for s in slabs],
                             axis=0)
        return lax.dot_general(kc, qt, (((1,), (1,)), ((), ())),
                               preferred_element_type=F32)

    def softmax_pv(s, c, t):
        cols = slice(t * qcol, (t + 1) * qcol)
        vtc = vt_ref[t * LANE:(t + 1) * LANE, c * tk:(c + 1) * tk]
        if not running_max:
            p = jnp.exp2(s).astype(BF16)
            acc_ref[:, cols] += jnp.dot(vtc, p, preferred_element_type=F32)
            return
        m_prev = m_ref[:, cols]
        m_new = jnp.maximum(m_prev, jnp.max(s, axis=0, keepdims=True))
        alpha = jnp.exp2(m_prev - m_new)
        p = jnp.exp2(s - m_new).astype(BF16)
        acc_ref[:, cols] = alpha * acc_ref[:, cols] + jnp.dot(vtc, p, preferred_element_type=F32)
        m_ref[:, cols] = m_new

    pending = [scores(*units[u]) for u in range(min(depth, len(units)))]
    for u, (c, t) in enumerate(units):
        if u + depth < len(units):
            pending.append(scores(*units[u + depth]))
        softmax_pv(pending.pop(0), c, t)

    @pl.when(j == pl.num_programs(2) - 1)
    def _():
        acc = acc_ref[...]
        o = (acc / acc[HEAD_DIM:HEAD_DIM + 1, :]).T
        lane = lax.broadcasted_iota(jnp.int32, (tq, LANE), 1)
        pairs = []
        for h in range(0, N_HEADS, 2):
            lo = o[h * tq:(h + 1) * tq, :]
            hi = pltpu.roll(o[(h + 1) * tq:(h + 2) * tq, :], HEAD_DIM, 1)
            pairs.append(jnp.where(lane < HEAD_DIM, lo, hi))
        o_ref[...] = jnp.concatenate(pairs, axis=1).astype(BF16)


def _attn_call(q, k, vt, running_max):
    B, L, _ = q.shape
    tq = min(ATT_TQ, L)
    tkb = min(ATT_TKB, L)
    tk, depth = ATT_CHUNK[running_max]
    tk = min(tk, tkb)
    kv_blocks = L // tkb
    kern = functools.partial(_attn_kernel, tq=tq, tk=tk, n_chunks=tkb // tk, depth=depth,
                             running_max=running_max)
    return pl.pallas_call(
        kern,
        grid=(B, L // tq, kv_blocks),
        in_specs=[
            pl.BlockSpec((None, tq, Q_DIM), lambda b, i, j: (b, i, 0)),
            pl.BlockSpec((None, tkb, KV_DIM), lambda b, i, j: (b, j, 0)),
            pl.BlockSpec((V_SLAB, tkb), lambda b, i, j: (0, b * kv_blocks + j)),
        ],
        out_specs=pl.BlockSpec((None, tq, Q_DIM), lambda b, i, j: (b, i, 0)),
        out_shape=jax.ShapeDtypeStruct((B, L, Q_DIM), BF16),
        scratch_shapes=[
            pltpu.VMEM((1, N_HEADS * tq), F32),
            pltpu.VMEM((LANE, N_HEADS * tq), F32),
        ],
        compiler_params=pltpu.CompilerParams(
            dimension_semantics=("parallel", "parallel", "arbitrary"),
            vmem_limit_bytes=VMEM_LIMIT_BYTES),
        name="flash_attn" if running_max else "flash_attn_bounded",
    )(q, k, vt)


def _mlp_body(x1, gm, w1_ref, w2_ref):
    h = _rms(x1, gm).astype(BF16)
    acc = jnp.zeros_like(x1)
    for c in range(D_FF // FF_CHUNK):
        cols = slice(c * FF_CHUNK, (c + 1) * FF_CHUNK)
        hid = jnp.maximum(jnp.dot(h, w1_ref[:, cols], preferred_element_type=F32), 0.0)
        acc = acc + jnp.dot((hid * hid).astype(BF16), w2_ref[cols, :], preferred_element_type=F32)
    return x1 + acc


def _attn_out_mlp_kernel(x_ref, o_ref, wo_ref, gm_ref, w1_ref, w2_ref, out_ref):
    for rows in _row_groups(x_ref.shape[0]):
        x1 = x_ref[rows, :] + jnp.dot(o_ref[rows, :], wo_ref[...], preferred_element_type=F32)
        out_ref[rows, :] = _mlp_body(x1, gm_ref[...], w1_ref, w2_ref)


def _mlp_final_kernel(x_ref, gm_ref, w1_ref, w2_ref, gf_ref, out_ref):
    for rows in _row_groups(x_ref.shape[0]):
        y = _mlp_body(x_ref[rows, :], gm_ref[...], w1_ref, w2_ref)
        out_ref[rows, :] = _rms(y, gf_ref[...])


def _row_call(kernel, name, T, row_inputs, const_inputs):
    tm = ROW_TILE
    row = lambda i: (i, 0)
    in_specs = [pl.BlockSpec((tm, a.shape[1]), row) for a in row_inputs]
    in_specs += [_const_spec(a.shape) for a in const_inputs]
    return pl.pallas_call(
        kernel,
        grid=(T // tm,),
        in_specs=in_specs,
        out_specs=pl.BlockSpec((tm, D_MODEL), row),
        out_shape=jax.ShapeDtypeStruct((T, D_MODEL), F32),
        compiler_params=pltpu.CompilerParams(
            dimension_semantics=("parallel",), vmem_limit_bytes=VMEM_LIMIT_BYTES),
        name=name,
    )(*row_inputs, *const_inputs)


def _conv_kernel(x_ref, xp_ref, xn_ref, g_ref, win_ref, cw_ref, wout_ref, out_ref, u_ref,
                 *, tm, tiles_per_seq):
    i = pl.program_id(0)
    first = (i % tiles_per_seq) == 0
    last = (i % tiles_per_seq) == tiles_per_seq - 1
    x = x_ref[...]
    xa = jnp.concatenate([xp_ref[...], x, xn_ref[...]], axis=0)
    h = _rms(xa, g_ref[...]).astype(BF16)
    bcx = jnp.dot(h, win_ref[...], preferred_element_type=F32)
    u = bcx[:, D_MODEL:2 * D_MODEL] * bcx[:, 2 * D_MODEL:]
    r = lax.broadcasted_iota(jnp.int32, (tm + 2 * SUBLANE, 1), 0)
    pad = (first & (r < SUBLANE)) | (last & (r >= tm + SUBLANE))
    u_ref[...] = jnp.where(pad, 0.0, u)
    z = (u_ref[pl.ds(SUBLANE - 1, tm), :] * cw_ref[0:1, :]
         + u_ref[pl.ds(SUBLANE, tm), :] * cw_ref[1:2, :]
         + u_ref[pl.ds(SUBLANE + 1, tm), :] * cw_ref[2:3, :])
    gated = (bcx[SUBLANE:SUBLANE + tm, :D_MODEL] * z).astype(BF16)
    out_ref[...] = x + jnp.dot(gated, wout_ref[...], preferred_element_type=F32)


def _conv_call(x2d, gain, w_in, conv_w, w_out, seq_len):
    T = x2d.shape[0]
    tm = ROW_TILE
    halo_blocks = tm // SUBLANE
    n_halo = T // SUBLANE
    kern = functools.partial(_conv_kernel, tm=tm, tiles_per_seq=seq_len // tm)
    return pl.pallas_call(
        kern,
        grid=(T // tm,),
        in_specs=[
            pl.BlockSpec((tm, D_MODEL), lambda i: (i, 0)),
            pl.BlockSpec((SUBLANE, D_MODEL), lambda i: (jnp.maximum(i * halo_blocks - 1, 0), 0)),
            pl.BlockSpec((SUBLANE, D_MODEL),
                         lambda i: (jnp.minimum((i + 1) * halo_blocks, n_halo - 1), 0)),
            _const_spec((1, D_MODEL)),
            _const_spec((D_MODEL, 3 * D_MODEL)),
            _const_spec((CONV_WIDTH, D_MODEL)),
            _const_spec((D_MODEL, D_MODEL)),
        ],
        out_specs=pl.BlockSpec((tm, D_MODEL), lambda i: (i, 0)),
        out_shape=jax.ShapeDtypeStruct((T, D_MODEL), F32),
        scratch_shapes=[pltpu.VMEM((tm + 2 * SUBLANE, D_MODEL), F32)],
        compiler_params=pltpu.CompilerParams(
            dimension_semantics=("parallel",), vmem_limit_bytes=VMEM_LIMIT_BYTES),
        name="gated_conv",
    )(x2d, x2d, x2d, gain, w_in, conv_w, w_out)


def _prep_attn_weights(w_qkv, q_gain, k_gain):
    g = np.arange(GQA_GROUP)[:, None, None]
    t = np.arange(N_KV_HEADS)[None, :, None]
    i = np.arange(HALF_DIM)[None, None, :]
    q0 = ((t * GQA_GROUP + g) * HEAD_DIM + 2 * i).reshape(-1)
    k0 = (Q_DIM + t * HEAD_DIM + 2 * i).reshape(-1)
    cols = np.concatenate([q0, q0 + 1, k0, k0 + 1])
    w_qk = jnp.take(w_qkv, jnp.asarray(cols), axis=1).astype(BF16)
    w_v = w_qkv[:, Q_DIM + KV_DIM:].reshape(D_MODEL, N_KV_HEADS, HEAD_DIM)
    w_v = jnp.pad(w_v, ((0, 0), (0, 0), (0, LANE - HEAD_DIM))).reshape(D_MODEL, V_SLAB)
    w_vt = w_v.T.astype(BF16)

    lane = np.arange(MXU_TILE)
    e_head = (lane[:, None] // HALF_DIM == lane[None, :] // HALF_DIM).astype(np.float32)
    reps = Q_HALF // HALF_DIM
    head_gains = jnp.stack([jnp.tile(q_gain[0::2], reps), jnp.tile(q_gain[1::2], reps),
                            jnp.tile(k_gain[0::2], reps), jnp.tile(k_gain[1::2], reps)])
    return w_qk, w_vt, jnp.asarray(e_head, BF16), head_gains


def _rope_tables(length):
    rows = length // GRID_W
    row_ids = jnp.repeat(jnp.arange(rows, dtype=F32), GRID_W)
    col_ids = jnp.tile(jnp.arange(GRID_W, dtype=F32), rows)
    inv_freq = ROPE_THETA ** (-jnp.arange(0, ROPE_AXIS_DIM, 2, dtype=F32) / ROPE_AXIS_DIM)
    ang = jnp.concatenate([row_ids[:, None] * inv_freq[None, :],
                           col_ids[:, None] * inv_freq[None, :]], axis=-1)
    reps = LANE // HALF_DIM
    return jnp.tile(jnp.cos(ang), (1, reps)), jnp.tile(jnp.sin(ang), (1, reps))


def _trunk(x, p):
    B, L, _ = x.shape
    T = B * L
    x2d = x.reshape(T, D_MODEL)
    cos, sin = _rope_tables(L)
    q, k, vt = _qkv_call(x2d, p["g_mix0"], p["w_qk"], p["w_vt"], p["e_head"], p["head_gains"],
                         cos, sin, L)
    q3, k3 = q.reshape(B, L, -1), k.reshape(B, L, -1)
    bounded = ((p["score_bound_log2"] <= SCORE_BOUND_LOG2)
               & (p["v_bound_log2"] <= F32_MAX_LOG2 - SCORE_BOUND_LOG2 - float(np.log2(L)) - 1.0))
    o = lax.cond(bounded,
                 lambda: _attn_call(q3, k3, vt, running_max=False),
                 lambda: _attn_call(q3, k3, vt, running_max=True))
    x2d = _row_call(_attn_out_mlp_kernel, "attn_out_mlp", T, [x2d, o.reshape(T, Q_DIM)],
                    [p["w_o"], p["g_mlp0"], p["w1_0"], p["w2_0"]])
    x2d = _conv_call(x2d, p["g_mix1"], p["w_in"], p["conv_w"], p["w_out"], L)
    y = _row_call(_mlp_final_kernel, "mlp_final_norm", T, [x2d],
                  [p["g_mlp1"], p["w1_1"], p["w2_1"], p["g_final"]])
    return y.reshape(B, L, D_MODEL)


def kernel(x_prompt, x_sample, norm_mix, norm_mlp, w_qkv, q_gain, k_gain, w_o, w_conv_in, conv_w,
           w_conv_out, w_mlp1, w_mlp2, norm_final):
    w_qk, w_vt, e_head, head_gains = _prep_attn_weights(w_qkv[0], q_gain[0], k_gain[0])
    score_bound_log2 = (HEAD_DIM * Q_SCALE * BF16_ROUNDING_MARGIN
                        * jnp.max(jnp.abs(q_gain[0])) * jnp.max(jnp.abs(k_gain[0])))
    w_v_col_norm = jnp.sqrt(jnp.max(jnp.sum(jnp.square(w_qkv[0][:, Q_DIM + KV_DIM:]), axis=0)))
    v_bound_log2 = jnp.log2(D_MODEL ** 0.5 * BF16_ROUNDING_MARGIN
                            * jnp.max(jnp.abs(norm_mix[0])) * w_v_col_norm)
    p = {
        "g_mix0": norm_mix[0:1], "g_mix1": norm_mix[1:2],
        "g_mlp0": norm_mlp[0:1], "g_mlp1": norm_mlp[1:2],
        "g_final": norm_final[None, :],
        "score_bound_log2": score_bound_log2, "v_bound_log2": v_bound_log2,
        "w_qk": w_qk, "w_vt": w_vt, "e_head": e_head, "head_gains": head_gains,
        "w_o": w_o[0].astype(BF16),
        "w1_0": w_mlp1[0].astype(BF16), "w2_0": w_mlp2[0].astype(BF16),
        "w1_1": w_mlp1[1].astype(BF16), "w2_1": w_mlp2[1].astype(BF16),
        "w_in": w_conv_in[0].astype(BF16), "conv_w": conv_w[0],
        "w_out": w_conv_out[0].astype(BF16),
    }
    return (_trunk(x_prompt, p), _trunk(x_sample, p))
```
